```python
import jax, jax.numpy as jnp
from jax import lax
import numpy as np

D_MODEL = 1024
BATCH = 4
SEQ = 8192
DEPTH = 2

GRID_W = 64
CTX_LEN = 256
D_MIX = D_MODEL
D_FOURIER = D_MIX // 2
FOURIER_GROUPS = 8
FOURIER_GW = D_FOURIER // FOURIER_GROUPS
D_MLSTM = D_MIX - D_FOURIER
MLSTM_HEADS = 4
MLSTM_HD = D_MLSTM // MLSTM_HEADS
N_GATE = 4 * MLSTM_HEADS
D_IN = D_FOURIER + 2 * D_MLSTM + N_GATE
CONV_K = 3
CHUNK = 128
D_FF = 2816
N_MOD = 9
FFN_RES = 0.5
EPS = 1e-6

kernel_name = "hybrid_fourier_mlstm_macaron_dit"


def rmsnorm(x, w):
    xf = x.astype(jnp.float32)
    y = xf * lax.rsqrt(jnp.mean(xf * xf, axis=-1, keepdims=True) + EPS)
    return (y * w.astype(jnp.float32)).astype(x.dtype)


def _mod(m, j):
    return m[..., 3 * j, :], m[..., 3 * j + 1, :], m[..., 3 * j + 2, :]


def _pre(x, nw, shift, scale):
    return rmsnorm(x, nw) * (1 + scale) + shift


def swiglu(h, w_in_ff, w_out_ff):
    gu = h @ w_in_ff
    g, u = gu[..., :D_FF], gu[..., D_FF:]
    return (jax.nn.silu(g) * u) @ w_out_ff


def _ffn_sublayer(x, mod, j, nw_pre, nw_post, w_in_ff, w_out_ff):
    shift, scale, gate = _mod(mod, j)
    y = swiglu(_pre(x, nw_pre, shift, scale), w_in_ff, w_out_ff)
    return x + FFN_RES * gate * rmsnorm(y, nw_post)


def fourier_mix(xf, w_fmix):
    b_, s_, _ = xf.shape
    xg = xf.astype(jnp.float32).reshape(b_, s_, FOURIER_GROUPS, FOURIER_GW)
    y = jnp.fft.fft2(xg, axes=(1, 3), norm="ortho").real.astype(xf.dtype)
    y = jnp.einsum("bsgc,gcd->bsgd", y, w_fmix)
    return y.reshape(b_, s_, D_FOURIER)


def grid_conv(x, w, b):
    b_, s_, c_ = x.shape
    rows = s_ // GRID_W
    img = x.reshape(b_, rows, GRID_W, c_)
    y = lax.conv_general_dilated(img, w[:, :, None, :].astype(x.dtype), (1, 1), "SAME",
                                 dimension_numbers=("NHWC", "HWIO", "NHWC"),
                                 feature_group_count=c_)
    return y.reshape(b_, s_, c_) + b


def seq_conv(x, w, b):
    c_ = x.shape[-1]
    y = lax.conv_general_dilated(x, w[:, None, :].astype(x.dtype), (1,), "SAME",
                                 dimension_numbers=("NWC", "WIO", "NWC"),
                                 feature_group_count=c_)
    return y + b


def _split_proj(u):
    xf = u[..., :D_FOURIER]
    xm = u[..., D_FOURIER:D_FOURIER + D_MLSTM]
    z = u[..., D_FOURIER + D_MLSTM:D_FOURIER + 2 * D_MLSTM]
    g = u[..., D_FOURIER + 2 * D_MLSTM:]
    return xf, xm, z, g


def _mlstm_inputs(xm, xconv, g, w_qkv, gate_b):
    b_, s_, _ = xm.shape
    heads = lambda t: t.reshape(b_, s_, MLSTM_HEADS, MLSTM_HD)
    q = jnp.einsum("bshd,hde->bhse", heads(xconv), w_qkv[0]).astype(jnp.float32)
    k = (jnp.einsum("bshd,hde->bhse", heads(xconv), w_qkv[1]) * MLSTM_HD ** -0.5).astype(jnp.float32)
    v = jnp.einsum("bshd,hde->bhse", heads(xm), w_qkv[2]).astype(jnp.float32)
    gp = g.astype(jnp.float32) + gate_b.reshape(-1).astype(jnp.float32)
    gp = jnp.transpose(gp.reshape(b_, s_, 4, MLSTM_HEADS), (2, 0, 3, 1))
    gates = (gp[0], jax.nn.log_sigmoid(gp[1]), gp[2], jax.nn.log_sigmoid(gp[3]))
    return q, k, v, gates


def _flip(t):
    return jnp.flip(t, axis=2)


def _chunk(t):
    return t.reshape(t.shape[:2] + (t.shape[2] // CHUNK, CHUNK) + t.shape[3:])


def mlstm_states(k, v, ig, lf, state0):
    kc, vc = _chunk(k), _chunk(v)
    igc, lfc = _chunk(ig), _chunk(lf)
    b = jnp.cumsum(lfc, axis=-1)
    b_tot = b[..., -1]
    a = b_tot[..., None] - b + igc
    m_loc = jnp.max(a, axis=-1)
    w = jnp.exp(a - m_loc[..., None])
    c_loc = jnp.einsum("bhcsd,bhcse->bhcde", vc * w[..., None], kc)
    n_loc = jnp.einsum("bhcs,bhcse->bhce", w, kc)

    def step(carry, xs):
        c_st, n_st, m_st = carry
        cl, nl, ml, bt = xs
        m_new = jnp.maximum(bt + m_st, ml)
        dec = jnp.exp(bt + m_st - m_new)
        inj = jnp.exp(ml - m_new)
        c_new = dec[..., None, None] * c_st + inj[..., None, None] * cl
        n_new = dec[..., None] * n_st + inj[..., None] * nl
        return (c_new, n_new, m_new), (c_st, n_st, m_st)

    xs = (jnp.moveaxis(c_loc, 2, 0), jnp.moveaxis(n_loc, 2, 0),
          jnp.moveaxis(m_loc, 2, 0), jnp.moveaxis(b_tot, 2, 0))
    final, prev = lax.scan(step, state0, xs)
    prev = tuple(jnp.moveaxis(t, 0, 2) for t in prev)
    return prev, final


def mlstm_outputs(q, k, v, ig, lf, prev):
    c_prev, n_prev, m_prev = prev
    qc, kc, vc = _chunk(q), _chunk(k), _chunk(v)
    igc, lfc = _chunk(ig), _chunk(lf)
    b = jnp.cumsum(lfc, axis=-1)
    log_inter = b + m_prev[..., None]
    log_intra = b[..., :, None] - b[..., None, :] + igc[..., None, :]
    scan_order = jnp.tril(jnp.ones((CHUNK, CHUNK), dtype=bool))
    log_intra = jnp.where(scan_order, log_intra, -jnp.inf)
    m_t = jnp.maximum(log_inter, jnp.max(log_intra, axis=-1))
    w_intra = jnp.exp(log_intra - m_t[..., None])
    w_inter = jnp.exp(log_inter - m_t)
    s = jnp.einsum("bhctd,bhcsd->bhcts", qc, kc) * w_intra
    num = (jnp.einsum("bhcts,bhcsd->bhctd", s, vc)
           + w_inter[..., None] * jnp.einsum("bhcde,bhcte->bhctd", c_prev, qc))
    den = jnp.sum(s, axis=-1) + w_inter * jnp.einsum("bhce,bhcte->bhct", n_prev, qc)
    h = num / jnp.maximum(jnp.abs(den), jnp.exp(-m_t))[..., None]
    return h.reshape(q.shape)


def _bidir_outputs(q, k, v, gates, prev_f, prev_b):
    i_f, f_f, i_b, f_b = gates
    h_f = mlstm_outputs(q, k, v, i_f, f_f, prev_f)
    h_b = mlstm_outputs(_flip(q), _flip(k), _flip(v), _flip(i_b), _flip(f_b), prev_b)
    return h_f + _flip(h_b)


def head_norm(h, w):
    mu = jnp.mean(h, axis=-1, keepdims=True)
    var = jnp.mean(jnp.square(h - mu), axis=-1, keepdims=True)
    hn = (h - mu) * lax.rsqrt(var + EPS)
    b_, _, s_, _ = h.shape
    hn = jnp.transpose(hn, (0, 2, 1, 3)).reshape(b_, s_, D_MLSTM)
    return hn * w.astype(jnp.float32)


def _combine(xf, h_sum, z, w_fmix, hn_w, w_out):
    yf = fourier_mix(xf, w_fmix)
    ym = head_norm(h_sum, hn_w).astype(z.dtype) * jax.nn.sigmoid(z)
    return jnp.concatenate([yf, ym], axis=-1) @ w_out


def setup_inputs(seed: int = 0) -> dict:
    key = jax.random.key(seed)
    ks = jax.random.split(key, 20)
    nrm = lambda k, shape, s: jax.random.normal(k, shape, jnp.float32) * s
    forget_bias = jnp.linspace(3.0, 6.0, MLSTM_HEADS, dtype=jnp.float32)
    gb = nrm(ks[14], (DEPTH, 4, MLSTM_HEADS), 0.1)
    gb = gb + jnp.stack([jnp.zeros_like(forget_bias), forget_bias,
                         jnp.zeros_like(forget_bias), forget_bias])[None]
    return {
        "x": nrm(ks[0], (BATCH, SEQ, D_MODEL), 1.0),
        "c": nrm(ks[1], (BATCH, D_MODEL), 1.0),
        "ctx": nrm(ks[2], (BATCH, CTX_LEN, D_MODEL), 1.0),
        "c_ctx": nrm(ks[3], (D_MODEL,), 1.0),
        "w_ada": nrm(ks[4], (DEPTH, D_MODEL, N_MOD * D_MODEL), 0.5 * D_MODEL ** -0.5),
        "b_ada": nrm(ks[5], (DEPTH, N_MOD * D_MODEL), 0.02),
        "norm_w": 1.0 + nrm(ks[6], (DEPTH, 6, D_MODEL), 0.05),
        "w_ff_in": nrm(ks[7], (DEPTH, 2, D_MODEL, 2 * D_FF), D_MODEL ** -0.5),
        "w_ff_out": nrm(ks[8], (DEPTH, 2, D_FF, D_MODEL), D_FF ** -0.5),
        "w_in": nrm(ks[9], (DEPTH, D_MODEL, D_IN), D_MODEL ** -0.5),
        "w_fmix": nrm(ks[10], (DEPTH, FOURIER_GROUPS, FOURIER_GW, FOURIER_GW), FOURIER_GW ** -0.5),
        "conv_w": nrm(ks[11], (DEPTH, CONV_K, CONV_K, D_MLSTM), (CONV_K * CONV_K) ** -0.5),
        "conv_b": nrm(ks[12], (DEPTH, D_MLSTM), 0.02),
        "w_qkv": nrm(ks[13], (DEPTH, 3, MLSTM_HEADS, MLSTM_HD, MLSTM_HD), MLSTM_HD ** -0.5),
        "gate_b": gb,
        "mlstm_norm_w": 1.0 + nrm(ks[15], (DEPTH, D_MLSTM), 0.05),
        "w_out": nrm(ks[16], (DEPTH, D_MIX, D_MODEL), D_MIX ** -0.5),
    }


def reference(x, c, ctx, c_ctx, w_ada, b_ada, norm_w, w_ff_in, w_ff_out, w_in, w_fmix,
              conv_w, conv_b, w_qkv, gate_b, mlstm_norm_w, w_out):
    b_ = x.shape[0]
    xl, xc = x, ctx
    zero = (jnp.zeros((b_, MLSTM_HEADS, MLSTM_HD, MLSTM_HD), jnp.float32),
            jnp.zeros((b_, MLSTM_HEADS, MLSTM_HD), jnp.float32),
            jnp.zeros((b_, MLSTM_HEADS), jnp.float32))
    for l in range(DEPTH):
        last = l == DEPTH - 1
        nw = norm_w[l]
        mod_l = (jax.nn.silu(c) @ w_ada[l] + b_ada[l]).reshape(b_, 1, N_MOD, D_MODEL)
        mod_c = (jax.nn.silu(c_ctx) @ w_ada[l] + b_ada[l]).reshape(N_MOD, D_MODEL)

        xl = _ffn_sublayer(xl, mod_l, 0, nw[0], nw[1], w_ff_in[l, 0], w_ff_out[l, 0])
        xc = _ffn_sublayer(xc, mod_c, 0, nw[0], nw[1], w_ff_in[l, 0], w_ff_out[l, 0])

        sh_l, sc_l, g_l = _mod(mod_l, 1)
        sh_c, sc_c, g_c = _mod(mod_c, 1)
        ul = _pre(xl, nw[2], sh_l, sc_l) @ w_in[l]
        uc = _pre(xc, nw[2], sh_c, sc_c) @ w_in[l]
        xf_l, xm_l, z_l, gp_l = _split_proj(ul)
        xf_c, xm_c, z_c, gp_c = _split_proj(uc)

        cv_l = jax.nn.silu(grid_conv(xm_l, conv_w[l], conv_b[l]))
        cv_c = jax.nn.silu(seq_conv(xm_c, conv_w[l, 1], conv_b[l]))
        q_l, k_l, v_l, gates_l = _mlstm_inputs(xm_l, cv_l, gp_l, w_qkv[l], gate_b[l])
        q_c, k_c, v_c, gates_c = _mlstm_inputs(xm_c, cv_c, gp_c, w_qkv[l], gate_b[l])

        prev_cf, st_f = mlstm_states(k_c, v_c, gates_c[0], gates_c[1], zero)
        prev_cb, st_b = mlstm_states(_flip(k_c), _flip(v_c), _flip(gates_c[2]),
                                     _flip(gates_c[3]), zero)
        prev_lf, _ = mlstm_states(k_l, v_l, gates_l[0], gates_l[1], st_f)
        prev_lb, _ = mlstm_states(_flip(k_l), _flip(v_l), _flip(gates_l[2]),
                                  _flip(gates_l[3]), st_b)
        h_l = _bidir_outputs(q_l, k_l, v_l, gates_l, prev_lf, prev_lb)
        y_l = _combine(xf_l, h_l, z_l, w_fmix[l], mlstm_norm_w[l], w_out[l])
        xl = xl + g_l * rmsnorm(y_l, nw[3])
        xl = _ffn_sublayer(xl, mod_l, 2, nw[4], nw[5], w_ff_in[l, 1], w_ff_out[l, 1])

        if not last:
            h_c = _bidir_outputs(q_c, k_c, v_c, gates_c, prev_cf, prev_cb)
            y_c = _combine(xf_c, h_c, z_c, w_fmix[l], mlstm_norm_w[l], w_out[l])
            xc = xc + g_c * rmsnorm(y_c, nw[3])
            xc = _ffn_sublayer(xc, mod_c, 2, nw[4], nw[5], w_ff_in[l, 1], w_ff_out[l, 1])
    return xl
```

```python
import functools
import math

import numpy as np
import jax
import jax.numpy as jnp
from jax import lax
from jax.experimental import pallas as pl
from jax.experimental.pallas import tpu as pltpu

D_MODEL = 1024
DEPTH = 2
GRID_W = 64
D_FOURIER = 512
FOURIER_GROUPS = 8
FOURIER_GW = 64
D_MLSTM = 512
MLSTM_HEADS = 4
MLSTM_HD = 128
CHUNK = 128
D_FF = 2816
N_MOD = 9
FFN_RES = 0.5
EPS = 1e-6

LANES = 128
SUBLANES = 8
MXU_DIM = 256
VMEM_BYTES = 64 * 1024 * 1024
N_HD = 2 * MLSTM_HEADS
STATE_W = 2 * MLSTM_HD
GATE_W = 2 * LANES
D_PROJ = 2 * D_FOURIER + 2 * D_MLSTM + GATE_W

F32 = jnp.float32
BF16 = jnp.bfloat16


def _cparams(n_grid, vmem_mb):
    return pltpu.CompilerParams(
        dimension_semantics=("arbitrary",) * n_grid,
        vmem_limit_bytes=min(vmem_mb * 1024 * 1024, VMEM_BYTES - 4 * 1024 * 1024))


def _dot(a, b):
    return jnp.dot(a, b, preferred_element_type=F32)


def _rms(x, w):
    return x * lax.rsqrt(jnp.mean(x * x, axis=-1, keepdims=True) + EPS) * w


def _mod3(m, j):
    return m[3 * j:3 * j + 1], m[3 * j + 1:3 * j + 2], m[3 * j + 2:3 * j + 3]


def _mod_index(mod):
    if mod.shape[0] == 1:
        return lambda b, i: (0, 0, 0)
    return lambda b, i: (b, 0, 0)


def _mod_kernel(c_ref, w_ref, b_ref, o_ref):
    c = c_ref[...]
    s = (c * jax.nn.sigmoid(c)).astype(BF16)
    o_ref[...] = _dot(s, w_ref[...].astype(BF16)) + b_ref[...]


def _modulation(cc, w_ada, b_ada):
    d = D_MODEL
    return pl.pallas_call(
        _mod_kernel,
        grid=(DEPTH, N_MOD),
        in_specs=[pl.BlockSpec((SUBLANES, d), lambda l, n: (0, 0)),
                  pl.BlockSpec((None, d, d), lambda l, n: (l, 0, n)),
                  pl.BlockSpec((None, 1, d), lambda l, n: (l, 0, n))],
        out_specs=pl.BlockSpec((None, SUBLANES, d), lambda l, n: (l, 0, n)),
        out_shape=jax.ShapeDtypeStruct((DEPTH, SUBLANES, N_MOD * d), F32),
        compiler_params=_cparams(2, 32),
        name="modulation",
    )(cc, w_ada, b_ada.reshape(DEPTH, 1, N_MOD * d))


FF_CHUNK = MXU_DIM
N_FF_CHUNKS = D_FF // FF_CHUNK


def _ffn_kernel(j, x_ref, mod_ref, nw_ref, wg_ref, wu_ref, wo_ref, o_ref, a_ref):
    x = x_ref[...]
    shift, scale, gate = _mod3(mod_ref[...], j)
    h = (_rms(x, nw_ref[2 * j:2 * j + 1]) * (1.0 + scale) + shift).astype(BF16)
    for c in range(N_FF_CHUNKS):
        sl = slice(c * FF_CHUNK, (c + 1) * FF_CHUNK)
        g = _dot(h, wg_ref[:, sl])
        u = _dot(h, wu_ref[:, sl])
        a_ref[:, sl] = (g * jax.nn.sigmoid(g) * u).astype(BF16)
    y = _dot(a_ref[...], wo_ref[...])
    o_ref[...] = x + FFN_RES * gate * _rms(y, nw_ref[2 * j + 1:2 * j + 2])


def _ffn(x, mod, j, nw, wg, wu, wo):
    b, s, d = x.shape
    tm = min(512, s)
    kern = functools.partial(_ffn_kernel, j)
    tok = pl.BlockSpec((None, tm, d), lambda bi, i: (bi, i, 0))
    full = lambda shp: pl.BlockSpec(shp, lambda bi, i: (0,) * len(shp))
    return pl.pallas_call(
        kern,
        grid=(b, s // tm),
        in_specs=[tok,
                  pl.BlockSpec((None, N_MOD, d), _mod_index(mod)),
                  full(nw.shape), full(wg.shape), full(wu.shape), full(wo.shape)],
        out_specs=tok,
        out_shape=jax.ShapeDtypeStruct(x.shape, F32),
        scratch_shapes=[pltpu.VMEM((tm, D_FF), BF16)],
        compiler_params=_cparams(2, 60),
        name="ffn",
    )(x, mod, nw, wg, wu, wo)


def _fold_kernel(wf_ref, bdw_ref, bdc_ref, bds_ref, o_ref):
    hi = lax.Precision.HIGHEST
    wf = wf_ref[...]
    bdw = bdw_ref[...]
    gr = jnp.dot(bdc_ref[...], bdw, precision=hi, preferred_element_type=F32)
    gi = jnp.dot(bds_ref[...], bdw, precision=hi, preferred_element_type=F32)
    o_ref[:, :D_FOURIER] = jnp.dot(wf, gr, precision=hi, preferred_element_type=F32).astype(BF16)
    o_ref[:, D_FOURIER:] = jnp.dot(wf, gi, precision=hi, preferred_element_type=F32).astype(BF16)


def _channel_dft_blocks():
    c = np.arange(FOURIER_GW, dtype=np.float64)
    ang = 2.0 * np.pi * np.outer(c, c) / FOURIER_GW
    eye = np.eye(FOURIER_GROUPS)
    scale = 1.0 / math.sqrt(FOURIER_GW)
    bdc = np.kron(eye, np.cos(ang) * scale)
    bds = np.kron(eye, -np.sin(ang) * scale)
    return jnp.asarray(bdc, F32), jnp.asarray(bds, F32)


def _fold_fourier(w_in, w_fmix):
    d = D_MODEL
    eye = jnp.eye(FOURIER_GROUPS, dtype=F32)
    bdw = jnp.einsum("gh,lgcd->lgchd", eye, w_fmix).reshape(DEPTH, D_FOURIER, D_FOURIER)
    bdc, bds = _channel_dft_blocks()
    sq = pl.BlockSpec((D_FOURIER, D_FOURIER), lambda l: (0, 0))
    return pl.pallas_call(
        _fold_kernel,
        grid=(DEPTH,),
        in_specs=[pl.BlockSpec((None, d, D_FOURIER), lambda l: (l, 0, 0)),
                  pl.BlockSpec((None, D_FOURIER, D_FOURIER), lambda l: (l, 0, 0)),
                  sq, sq],
        out_specs=pl.BlockSpec((None, d, 2 * D_FOURIER), lambda l: (l, 0, 0)),
        out_shape=jax.ShapeDtypeStruct((DEPTH, d, 2 * D_FOURIER), BF16),
        compiler_params=_cparams(1, 32),
        name="fold_fourier",
    )(w_in, bdw, bdc, bds)


def _inproj_kernel(x_ref, mod_ref, nw_ref, w_ref, z_ref, xm_ref, zg_ref, g_ref):
    shift, scale, _ = _mod3(mod_ref[...], 1)
    h = (_rms(x_ref[...], nw_ref[2:3]) * (1.0 + scale) + shift).astype(BF16)
    o0 = 2 * D_FOURIER
    o1 = o0 + D_MLSTM
    o2 = o1 + D_MLSTM
    z_ref[...] = _dot(h, w_ref[:, :o0]).astype(BF16)
    xm_ref[...] = _dot(h, w_ref[:, o0:o1])
    zg_ref[...] = _dot(h, w_ref[:, o1:o2])
    g_ref[...] = _dot(h, w_ref[:, o2:])


def _inproj(x, mod, nw, wcat):
    b, s, d = x.shape
    tm = min(512, s)
    tok = lambda w: pl.BlockSpec((None, tm, w), lambda bi, i: (bi, i, 0))
    full = lambda shp: pl.BlockSpec(shp, lambda bi, i: (0,) * len(shp))
    return pl.pallas_call(
        _inproj_kernel,
        grid=(b, s // tm),
        in_specs=[tok(d), pl.BlockSpec((None, N_MOD, d), _mod_index(mod)),
                  full(nw.shape), full(wcat.shape)],
        out_specs=[tok(2 * D_FOURIER), tok(D_MLSTM), tok(D_MLSTM), tok(GATE_W)],
        out_shape=[jax.ShapeDtypeStruct((b, s, 2 * D_FOURIER), BF16),
                   jax.ShapeDtypeStruct((b, s, D_MLSTM), F32),
                   jax.ShapeDtypeStruct((b, s, D_MLSTM), F32),
                   jax.ShapeDtypeStruct((b, s, GATE_W), F32)],
        compiler_params=_cparams(2, 48),
        name="inproj",
    )(x, mod, nw, wcat)


def _convqkv_kernel(width, rows, halo, prev_ref, cur_ref, next_ref, cw_ref, cb_ref,
                    wqk_ref, wv_ref, o_ref):
    i = pl.program_id(1)
    tm = cur_ref.shape[0]
    cur = cur_ref[...]
    prev = jnp.where(i > 0, prev_ref[...], 0.0)
    nxt = jnp.where(i < pl.num_programs(1) - 1, next_ref[...], 0.0)
    ext = jnp.concatenate([prev, cur, nxt], axis=0)
    n = ext.shape[0]
    r = lax.broadcasted_iota(jnp.int32, ext.shape, 0)
    col = (r + (width - halo % width)) & (width - 1)
    left = jnp.where(col == 0, 0.0, pltpu.roll(ext, 1, axis=0))
    right = jnp.where(col == width - 1, 0.0, pltpu.roll(ext, n - 1, axis=0))
    taps = (left, ext, right)
    cw = cw_ref[...]
    acc = jnp.zeros((tm, cur.shape[1]), F32) + cb_ref[...]
    for dy in (-1, 0, 1):
        if rows == 1 and dy != 0:
            continue
        lo = halo + dy * width
        for dx in (-1, 0, 1):
            k = (dy + 1) * 3 + (dx + 1)
            acc = acc + cw[k:k + 1] * taps[dx + 1][lo:lo + tm]
    cv = (acc * jax.nn.sigmoid(acc)).astype(BF16)
    xm = cur.astype(BF16)
    hd = MLSTM_HD
    for h in range(MLSTM_HEADS):
        sl = slice(h * hd, (h + 1) * hd)
        qk = _dot(cv[:, sl], wqk_ref[h])
        o_ref[:, sl] = qk[:, :hd].astype(BF16)
        o_ref[:, D_MLSTM + h * hd:D_MLSTM + (h + 1) * hd] = (qk[:, hd:] * hd ** -0.5).astype(BF16)
        o_ref[:, 2 * D_MLSTM + h * hd:2 * D_MLSTM + (h + 1) * hd] = _dot(xm[:, sl], wv_ref[h]).astype(BF16)


def _convqkv(xm, width, cw, cb, wqk, wv):
    b, s, c = xm.shape
    rows = s // width
    tm = min(512, s)
    halo = LANES if rows > 1 else SUBLANES
    assert tm % width == 0 and tm % halo == 0 and (rows == 1 or halo > width)
    assert width & (width - 1) == 0
    per = tm // halo
    nblk = s // halo
    kern = functools.partial(_convqkv_kernel, width, rows, halo)
    full = lambda shp: pl.BlockSpec(shp, lambda bi, i: (0,) * len(shp))
    return pl.pallas_call(
        kern,
        grid=(b, s // tm),
        in_specs=[pl.BlockSpec((None, halo, c), lambda bi, i: (bi, jnp.maximum(i * per - 1, 0), 0)),
                  pl.BlockSpec((None, tm, c), lambda bi, i: (bi, i, 0)),
                  pl.BlockSpec((None, halo, c), lambda bi, i: (bi, jnp.minimum((i + 1) * per, nblk - 1), 0)),
                  full(cw.shape), full(cb.shape), full(wqk.shape), full(wv.shape)],
        out_specs=pl.BlockSpec((None, tm, 3 * c), lambda bi, i: (bi, i, 0)),
        out_shape=jax.ShapeDtypeStruct((b, s, 3 * c), BF16),
        compiler_params=_cparams(2, 32),
        name="convqkv",
    )(xm, xm, xm, cw, cb, wqk, wv)


def _gate_scan(x, op, ident, lane_fwd):
    n = x.shape[0]
    t = lax.broadcasted_iota(jnp.int32, x.shape, 0)
    s = 1
    while s < n:
        down = jnp.where(t >= s, pltpu.roll(x, s, axis=0), ident)
        up = jnp.where(t < n - s, pltpu.roll(x, n - s, axis=0), ident)
        x = op(x, jnp.where(lane_fwd, down, up))
        s *= 2
    return x


def _mlstm_kernel(qf_ref, qb_ref, gf_ref, gb_ref, bias_ref, c0_ref, m0_ref,
                  hf_ref, hb_ref, cfin_ref, mfin_ref, c_st, m_st):
    j = pl.program_id(1)
    L = CHUNK
    hd = MLSTM_HD

    @pl.when(j == 0)
    def _():
        c_st[...] = c0_ref[...]
        m_st[...] = m0_ref[...]

    lane = lax.broadcasted_iota(jnp.int32, (L, LANES), 1)
    lane_fwd = lane < MLSTM_HEADS
    gp = jnp.where(lane_fwd, gf_ref[:, :LANES], gb_ref[:, :LANES]) + bias_ref[:, :LANES]
    fp = jnp.where(lane_fwd, gf_ref[:, LANES:], gb_ref[:, LANES:]) + bias_ref[:, LANES:]
    lf = jax.nn.log_sigmoid(fp)
    bc = _gate_scan(lf, jnp.add, 0.0, lane_fwd)
    gs = gp - bc
    cm = _gate_scan(gs, jnp.maximum, -jnp.inf, lane_fwd)
    lane1 = lane[:1]
    btot = jnp.where(lane1 < MLSTM_HEADS, bc[L - 1:L], bc[0:1])
    cmtot = jnp.where(lane1 < MLSTM_HEADS, cm[L - 1:L], cm[0:1])
    m_prev = m_st[...]
    xm = jnp.maximum(m_prev, cm)
    w_inter = jnp.exp(m_prev - xm)
    e_floor = jnp.exp(-(bc + xm))
    m_loc = btot + cmtot
    m_new = jnp.maximum(btot + m_prev, m_loc)
    dec = jnp.exp(btot + m_prev - m_new)
    inj = jnp.exp(m_loc - m_new)
    w_st = jnp.exp(gs - cmtot)
    gs_row = gs.T
    m_st[...] = m_new

    row = lax.broadcasted_iota(jnp.int32, (L, L), 0)
    colm = lax.broadcasted_iota(jnp.int32, (L, L), 1)
    ones_blk = jnp.where(lax.broadcasted_iota(jnp.int32, (L, hd), 1) == 0, 1.0, 0.0).astype(BF16)
    nt = (((1,), (1,)), ((), ()))
    for p in range(N_HD):
        d, hh = divmod(p, MLSTM_HEADS)
        src = qf_ref if d == 0 else qb_ref
        dst = hf_ref if d == 0 else hb_ref
        q = src[:, hh * hd:(hh + 1) * hd]
        k = src[:, D_MLSTM + hh * hd:D_MLSTM + (hh + 1) * hd]
        v = src[:, 2 * D_MLSTM + hh * hd:2 * D_MLSTM + (hh + 1) * hd]
        s_qk = lax.dot_general(q, k, nt, preferred_element_type=F32)
        seen = (colm <= row) if d == 0 else (colm >= row)
        arg = gs_row[p:p + 1, :] - xm[:, p:p + 1]
        decay = jnp.exp(jnp.where(seen, arg, -jnp.inf))
        pm = (s_qk * decay).astype(BF16)
        qw = (q.astype(F32) * w_inter[:, p:p + 1]).astype(BF16)
        vaug = jnp.concatenate([v, ones_blk], axis=1)
        c_prev = c_st[p]
        lhs = jnp.concatenate([pm, qw], axis=1)
        rhs = jnp.concatenate([vaug, c_prev.astype(BF16)], axis=0)
        o = _dot(lhs, rhs)
        num = o[:, :hd]
        den = o[:, hd:hd + 1]
        dst[:, hh * hd:(hh + 1) * hd] = num / jnp.maximum(jnp.abs(den), e_floor[:, p:p + 1])
        kwt = (k.astype(F32) * w_st[:, p:p + 1]).T.astype(BF16)
        c_loc = _dot(kwt, vaug)
        c_st[p] = dec[:, p:p + 1] * c_prev + inj[:, p:p + 1] * c_loc

    @pl.when(j == pl.num_programs(1) - 1)
    def _():
        cfin_ref[...] = c_st[...]
        mfin_ref[...] = m_st[...]


def _mlstm(qkv, gates, bias, c0, m0):
    b, s, _ = qkv.shape
    nc = s // CHUNK
    fwd = lambda bi, j: (bi, j, 0)
    bwd = lambda bi, j: (bi, nc - 1 - j, 0)
    st_c = pl.BlockSpec((None, N_HD, MLSTM_HD, STATE_W), lambda bi, j: (bi, 0, 0, 0))
    st_m = pl.BlockSpec((None, 1, LANES), lambda bi, j: (bi, 0, 0))
    return pl.pallas_call(
        _mlstm_kernel,
        grid=(b, nc),
        in_specs=[pl.BlockSpec((None, CHUNK, 3 * D_MLSTM), fwd),
                  pl.BlockSpec((None, CHUNK, 3 * D_MLSTM), bwd),
                  pl.BlockSpec((None, CHUNK, GATE_W), fwd),
                  pl.BlockSpec((None, CHUNK, GATE_W), bwd),
                  pl.BlockSpec((1, GATE_W), lambda bi, j: (0, 0)),
                  st_c, st_m],
        out_specs=[pl.BlockSpec((None, CHUNK, D_MLSTM), fwd),
                   pl.BlockSpec((None, CHUNK, D_MLSTM), bwd),
                   st_c, st_m],
        out_shape=[jax.ShapeDtypeStruct((b, s, D_MLSTM), F32),
                   jax.ShapeDtypeStruct((b, s, D_MLSTM), F32),
                   jax.ShapeDtypeStruct(c0.shape, F32),
                   jax.ShapeDtypeStruct(m0.shape, F32)],
        scratch_shapes=[pltpu.VMEM((N_HD, MLSTM_HD, STATE_W), F32),
                        pltpu.VMEM((1, LANES), F32)],
        compiler_params=_cparams(2, 32),
        name="mlstm",
    )(qkv, qkv, gates, gates, bias, c0, m0)


DFT_GROUP = SUBLANES


def _dft_tables(s, n2):
    n1 = s // n2
    k1 = np.arange(n1, dtype=np.float64)
    s1 = np.arange(n1, dtype=np.float64)
    ma = np.zeros((n2, 2 * n1, 2 * n1), np.float64)
    for s2 in range(n2):
        th = 2.0 * np.pi * np.outer(k1, s1 * n2 + s2) / s
        c, sn = np.cos(th), np.sin(th)
        ma[s2] = np.block([[c, sn], [-sn, c]])
    k2 = np.arange(n2, dtype=np.float64)
    ph = 2.0 * np.pi * np.outer(k2, k2) / n2
    eye = np.eye(DFT_GROUP)
    scale = 1.0 / math.sqrt(s)
    lb = np.einsum("ksr,ab->kasrb", np.stack([np.cos(ph), np.sin(ph)], axis=-1) * scale, eye)
    lb = lb.reshape(n2 * DFT_GROUP, n2 * 2 * DFT_GROUP)
    return jnp.asarray(ma, F32).astype(BF16), jnp.asarray(lb, F32).astype(BF16)


def _dft_a_kernel(ga, z_ref, ma_ref, t_ref):
    w = 2 * D_FOURIER
    for s in range(ga):
        zr = z_ref[:, s * w:s * w + D_FOURIER]
        zi = z_ref[:, s * w + D_FOURIER:(s + 1) * w]
        t_ref[s] = _dot(ma_ref[s], jnp.concatenate([zr, zi], axis=0))


def _dft_b_kernel(t_ref, lb_ref, y_ref):
    n2 = t_ref.shape[0]
    rhs = t_ref[...].reshape(n2 * 2 * DFT_GROUP, D_FOURIER).astype(BF16)
    y_ref[...] = _dot(lb_ref[...], rhs).reshape(n2, DFT_GROUP, D_FOURIER)


def _seq_dft(z, n2=64):
    b, s, w = z.shape
    n1 = s // n2
    ga = min(8, n2)
    ma, lb = _dft_tables(s, n2)
    t = pl.pallas_call(
        functools.partial(_dft_a_kernel, ga),
        grid=(b, n2 // ga),
        in_specs=[pl.BlockSpec((None, n1, ga * w), lambda bi, j: (bi, 0, j)),
                  pl.BlockSpec((ga, 2 * n1, 2 * n1), lambda bi, j: (j, 0, 0))],
        out_specs=pl.BlockSpec((None, ga, 2 * n1, D_FOURIER), lambda bi, j: (bi, j, 0, 0)),
        out_shape=jax.ShapeDtypeStruct((b, n2, 2 * n1, D_FOURIER), F32),
        compiler_params=_cparams(2, 48),
        name="dft_stage1",
    )(z.reshape(b, n1, n2 * w), ma)
    y = pl.pallas_call(
        _dft_b_kernel,
        grid=(b, n1 // DFT_GROUP),
        in_specs=[pl.BlockSpec((None, n2, 2, DFT_GROUP, D_FOURIER), lambda bi, g: (bi, 0, 0, g, 0)),
                  pl.BlockSpec(lb.shape, lambda bi, g: (0, 0))],
        out_specs=pl.BlockSpec((None, n2, DFT_GROUP, D_FOURIER), lambda bi, g: (bi, 0, g, 0)),
        out_shape=jax.ShapeDtypeStruct((b, n2, n1, D_FOURIER), F32),
        compiler_params=_cparams(2, 32),
        name="dft_stage2",
    )(t.reshape(b, n2, 2, n1, D_FOURIER), lb)
    return y.reshape(b, s, D_FOURIER)


def _dense_dft_kernel(z_ref, ld_ref, y_ref):
    rhs = jnp.concatenate([z_ref[:, :D_FOURIER], z_ref[:, D_FOURIER:]], axis=0)
    y_ref[...] = _dot(ld_ref[...], rhs)


def _dense_dft(z):
    b, s, w = z.shape
    k = np.arange(s, dtype=np.float64)
    th = 2.0 * np.pi * np.outer(k, k) / s
    ld = jnp.asarray(np.concatenate([np.cos(th), np.sin(th)], axis=1) / math.sqrt(s), F32).astype(BF16)
    return pl.pallas_call(
        _dense_dft_kernel,
        grid=(b,),
        in_specs=[pl.BlockSpec((None, s, w), lambda bi: (bi, 0, 0)),
                  pl.BlockSpec(ld.shape, lambda bi: (0, 0))],
        out_specs=pl.BlockSpec((None, s, D_FOURIER), lambda bi: (bi, 0, 0)),
        out_shape=jax.ShapeDtypeStruct((b, s, D_FOURIER), F32),
        compiler_params=_cparams(1, 32),
        name="dft_dense",
    )(z, ld)


def _combine_kernel(x_ref, yf_ref, hf_ref, hb_ref, zg_ref, mod_ref, nw_ref, hnw_ref, wo_ref, o_ref):
    _, _, gate = _mod3(mod_ref[...], 1)
    hs = hf_ref[...] + hb_ref[...]
    hd = MLSTM_HD
    parts = []
    for h in range(MLSTM_HEADS):
        seg = hs[:, h * hd:(h + 1) * hd]
        mu = jnp.mean(seg, axis=-1, keepdims=True)
        cen = seg - mu
        var = jnp.mean(cen * cen, axis=-1, keepdims=True)
        parts.append(cen * lax.rsqrt(var + EPS))
    hn = jnp.concatenate(parts, axis=1) * hnw_ref[...]
    ym = (hn * jax.nn.sigmoid(zg_ref[...])).astype(BF16)
    y = _dot(yf_ref[...].astype(BF16), wo_ref[:D_FOURIER]) + _dot(ym, wo_ref[D_FOURIER:])
    o_ref[...] = x_ref[...] + gate * _rms(y, nw_ref[3:4])


def _combine(x, yf, hf, hb, zg, mod, nw, hnw, wo):
    b, s, d = x.shape
    tm = min(512, s)
    tok = lambda w: pl.BlockSpec((None, tm, w), lambda bi, i: (bi, i, 0))
    full = lambda shp: pl.BlockSpec(shp, lambda bi, i: (0,) * len(shp))
    return pl.pallas_call(
        _combine_kernel,
        grid=(b, s // tm),
        in_specs=[tok(d), tok(D_FOURIER), tok(D_MLSTM), tok(D_MLSTM), tok(D_MLSTM),
                  pl.BlockSpec((None, N_MOD, d), _mod_index(mod)),
                  full(nw.shape), full(hnw.shape), full(wo.shape)],
        out_specs=tok(d),
        out_shape=jax.ShapeDtypeStruct(x.shape, F32),
        compiler_params=_cparams(2, 40),
        name="combine",
    )(x, yf, hf, hb, zg, mod, nw, hnw, wo)


def _gate_layout(w_gate, gate_b):
    h = MLSTM_HEADS
    d = w_gate.shape[0]
    pad_w = jnp.zeros((d, LANES - N_HD), w_gate.dtype)
    w = jnp.concatenate([w_gate[:, 0:h], w_gate[:, 2 * h:3 * h], pad_w,
                         w_gate[:, h:2 * h], w_gate[:, 3 * h:4 * h], pad_w], axis=1)
    pad_b = jnp.zeros((LANES - N_HD,), gate_b.dtype)
    bias = jnp.concatenate([gate_b[0], gate_b[2], pad_b, gate_b[1], gate_b[3], pad_b])
    return w, bias.reshape(1, GATE_W)


def kernel(x, c, ctx, c_ctx, w_ada, b_ada, norm_w, w_ff_in, w_ff_out, w_in, w_fmix,
           conv_w, conv_b, w_qkv, gate_b, mlstm_norm_w, w_out):
    b = x.shape[0]
    d = D_MODEL
    n_cond = b + 1
    assert n_cond <= SUBLANES
    cc = jnp.concatenate([c, c_ctx[None], jnp.zeros((SUBLANES - n_cond, d), F32)], axis=0)
    mod = _modulation(cc, w_ada, b_ada).reshape(DEPTH, SUBLANES, N_MOD, d)
    w_fold = _fold_fourier(w_in, w_fmix)

    xl, xc = x, ctx
    c_zero = jnp.zeros((b, N_HD, MLSTM_HD, STATE_W), F32)
    m_zero = jnp.zeros((b, 1, LANES), F32)
    for l in range(DEPTH):
        last = l == DEPTH - 1
        mod_l = mod[l, :b]
        mod_c = mod[l, b:b + 1]
        nw = norm_w[l]
        ffw = [(w_ff_in[l, i, :, :D_FF].astype(BF16), w_ff_in[l, i, :, D_FF:].astype(BF16),
                w_ff_out[l, i].astype(BF16)) for i in range(2)]
        o_xm = D_FOURIER
        o_g = D_FOURIER + 2 * D_MLSTM
        w_gate, g_bias = _gate_layout(w_in[l, :, o_g:], gate_b[l])
        wcat = jnp.concatenate([w_fold[l], w_in[l, :, o_xm:o_g].astype(BF16), w_gate.astype(BF16)], axis=1)
        cw = conv_w[l].reshape(9, D_MLSTM)
        cb = conv_b[l].reshape(1, D_MLSTM)
        wqk = jnp.concatenate([w_qkv[l, 0], w_qkv[l, 1]], axis=-1).astype(BF16)
        wv = w_qkv[l, 2].astype(BF16)
        hnw = mlstm_norm_w[l].reshape(1, D_MLSTM)
        wo = w_out[l].astype(BF16)

        xl = _ffn(xl, mod_l, 0, nw, *ffw[0])
        xc = _ffn(xc, mod_c, 0, nw, *ffw[0])
        z_l, xm_l, zg_l, g_l = _inproj(xl, mod_l, nw, wcat)
        z_c, xm_c, zg_c, g_c = _inproj(xc, mod_c, nw, wcat)
        qkv_l = _convqkv(xm_l, GRID_W, cw, cb, wqk, wv)
        qkv_c = _convqkv(xm_c, xm_c.shape[1], cw, cb, wqk, wv)
        hf_c, hb_c, c_ctx_st, m_ctx_st = _mlstm(qkv_c, g_c, g_bias, c_zero, m_zero)
        hf_l, hb_l, _, _ = _mlstm(qkv_l, g_l, g_bias, c_ctx_st, m_ctx_st)
        yf_l = _seq_dft(z_l)
        xl = _combine(xl, yf_l, hf_l, hb_l, zg_l, mod_l, nw, hnw, wo)
        xl = _ffn(xl, mod_l, 2, nw, *ffw[1])
        if not last:
            yf_c = _dense_dft(z_c)
            xc = _combine(xc, yf_c, hf_c, hb_c, zg_c, mod_c, nw, hnw, wo)
            xc = _ffn(xc, mod_c, 2, nw, *ffw[1])
    return xl
```

```python
import functools
import math

import numpy as np
import jax
import jax.numpy as jnp
from jax import lax
from jax.experimental import pallas as pl
from jax.experimental.pallas import tpu as pltpu

D_MODEL = 1024
DEPTH = 2
GRID_W = 64
D_FOURIER = 512
FOURIER_GROUPS = 8
FOURIER_GW = 64
D_MLSTM = 512
MLSTM_HEADS = 4
MLSTM_HD = 128
CHUNK = 128
D_FF = 2816
N_MOD = 9
FFN_RES = 0.5
EPS = 1e-6

LANES = 128
SUBLANES = 8
PACKED_SUBLANES = 16
MXU_DIM = 256
VMEM_BYTES = 64 * 1024 * 1024
N_HD = 2 * MLSTM_HEADS
STATE_ROWS = MLSTM_HD + PACKED_SUBLANES
GATE_ROWS = 2 * N_HD
SCAN_CHUNKS = 2

F32 = jnp.float32
BF16 = jnp.bfloat16
NT = (((1,), (1,)), ((), ()))


def _cparams(n_grid, vmem_mb):
    return pltpu.CompilerParams(
        dimension_semantics=("arbitrary",) * n_grid,
        vmem_limit_bytes=min(vmem_mb * 1024 * 1024, VMEM_BYTES - 4 * 1024 * 1024))


def _dot(a, b):
    return jnp.dot(a, b, preferred_element_type=F32)


def _dot_nt(a, b):
    return lax.dot_general(a, b, NT, preferred_element_type=F32)


def _rms(x, w):
    return x * lax.rsqrt(jnp.mean(x * x, axis=-1, keepdims=True) + EPS) * w


def _mod3(m, j):
    return m[3 * j:3 * j + 1], m[3 * j + 1:3 * j + 2], m[3 * j + 2:3 * j + 3]


def _mod_index(mod):
    if mod.shape[0] == 1:
        return lambda b, i: (0, 0, 0)
    return lambda b, i: (b, 0, 0)


def _full(shape):
    return pl.BlockSpec(shape, lambda *_: (0,) * len(shape))


def _mod_kernel(c_ref, w_ref, b_ref, o_ref):
    c = c_ref[...]
    s = (c * jax.nn.sigmoid(c)).astype(BF16)
    o_ref[...] = _dot(s, w_ref[...].astype(BF16)) + b_ref[...]


def _modulation(cc, w_ada, b_ada):
    d = D_MODEL
    return pl.pallas_call(
        _mod_kernel,
        grid=(DEPTH, N_MOD),
        in_specs=[pl.BlockSpec((SUBLANES, d), lambda l, n: (0, 0)),
                  pl.BlockSpec((None, d, d), lambda l, n: (l, 0, n)),
                  pl.BlockSpec((None, 1, d), lambda l, n: (l, 0, n))],
        out_specs=pl.BlockSpec((None, SUBLANES, d), lambda l, n: (l, 0, n)),
        out_shape=jax.ShapeDtypeStruct((DEPTH, SUBLANES, N_MOD * d), F32),
        compiler_params=_cparams(2, 32),
        name="modulation",
    )(cc, w_ada, b_ada.reshape(DEPTH, 1, N_MOD * d))


FF_CHUNK = MXU_DIM
N_FF_CHUNKS = D_FF // FF_CHUNK


def _ffn_kernel(j, x_ref, mod_ref, nw_ref, wg_ref, wu_ref, wo_ref, o_ref, a_ref):
    x = x_ref[...]
    shift, scale, gate = _mod3(mod_ref[...], j)
    h = (_rms(x, nw_ref[2 * j:2 * j + 1]) * (1.0 + scale) + shift).astype(BF16)
    for c in range(N_FF_CHUNKS):
        sl = slice(c * FF_CHUNK, (c + 1) * FF_CHUNK)
        g = _dot(h, wg_ref[:, sl])
        u = _dot(h, wu_ref[:, sl])
        a_ref[:, sl] = (g * jax.nn.sigmoid(g) * u).astype(BF16)
    y = _dot(a_ref[...], wo_ref[...])
    o_ref[...] = x + FFN_RES * gate * _rms(y, nw_ref[2 * j + 1:2 * j + 2])


def _ffn(x, mod, j, nw, wg, wu, wo):
    b, s, d = x.shape
    tm = min(512, s)
    kern = functools.partial(_ffn_kernel, j)
    tok = pl.BlockSpec((None, tm, d), lambda bi, i: (bi, i, 0))
    return pl.pallas_call(
        kern,
        grid=(b, s // tm),
        in_specs=[tok,
                  pl.BlockSpec((None, N_MOD, d), _mod_index(mod)),
                  _full(nw.shape), _full(wg.shape), _full(wu.shape), _full(wo.shape)],
        out_specs=tok,
        out_shape=jax.ShapeDtypeStruct(x.shape, F32),
        scratch_shapes=[pltpu.VMEM((tm, D_FF), BF16)],
        compiler_params=_cparams(2, 60),
        name="ffn",
    )(x, mod, nw, wg, wu, wo)


def _fold_kernel(wf_ref, bdw_ref, bdc_ref, bds_ref, o_ref):
    hi = lax.Precision.HIGHEST
    wf = wf_ref[...]
    bdw = bdw_ref[...]
    gr = jnp.dot(bdc_ref[...], bdw, precision=hi, preferred_element_type=F32)
    gi = jnp.dot(bds_ref[...], bdw, precision=hi, preferred_element_type=F32)
    o_ref[:, :D_FOURIER] = jnp.dot(wf, gr, precision=hi, preferred_element_type=F32).astype(BF16)
    o_ref[:, D_FOURIER:] = jnp.dot(wf, gi, precision=hi, preferred_element_type=F32).astype(BF16)


def _channel_dft_blocks():
    c = np.arange(FOURIER_GW, dtype=np.float64)
    ang = 2.0 * np.pi * np.outer(c, c) / FOURIER_GW
    eye = np.eye(FOURIER_GROUPS)
    scale = 1.0 / math.sqrt(FOURIER_GW)
    bdc = np.kron(eye, np.cos(ang) * scale)
    bds = np.kron(eye, -np.sin(ang) * scale)
    return jnp.asarray(bdc, F32), jnp.asarray(bds, F32)


def _fold_fourier(w_in, w_fmix):
    d = D_MODEL
    eye = jnp.eye(FOURIER_GROUPS, dtype=F32)
    bdw = jnp.einsum("gh,lgcd->lgchd", eye, w_fmix).reshape(DEPTH, D_FOURIER, D_FOURIER)
    bdc, bds = _channel_dft_blocks()
    sq = pl.BlockSpec((D_FOURIER, D_FOURIER), lambda l: (0, 0))
    return pl.pallas_call(
        _fold_kernel,
        grid=(DEPTH,),
        in_specs=[pl.BlockSpec((None, d, D_FOURIER), lambda l: (l, 0, 0)),
                  pl.BlockSpec((None, D_FOURIER, D_FOURIER), lambda l: (l, 0, 0)),
                  sq, sq],
        out_specs=pl.BlockSpec((None, d, 2 * D_FOURIER), lambda l: (l, 0, 0)),
        out_shape=jax.ShapeDtypeStruct((DEPTH, d, 2 * D_FOURIER), BF16),
        compiler_params=_cparams(1, 32),
        name="fold_fourier",
    )(w_in, bdw, bdc, bds)


def _inproj_kernel(x_ref, mod_ref, nw_ref, wz_ref, wxm_ref, wzt_ref, wg_ref,
                   z_ref, xm_ref, zt_ref, gt_ref):
    shift, scale, _ = _mod3(mod_ref[...], 1)
    h = (_rms(x_ref[...], nw_ref[2:3]) * (1.0 + scale) + shift).astype(BF16)
    z_ref[...] = _dot(h, wz_ref[...])
    xm_ref[...] = _dot(h, wxm_ref[...])
    zt_ref[...] = _dot_nt(wzt_ref[...], h)
    g = _dot(h, wg_ref[...])
    gt_ref[...] = jnp.concatenate([g[:, :LANES].T[:N_HD], g[:, LANES:].T[:N_HD]], axis=0)


def _inproj(x, mod, nw, wz, wxm, wzt, wg):
    b, s, d = x.shape
    tm = min(512, s)
    tok = lambda w: pl.BlockSpec((None, tm, w), lambda bi, i: (bi, i, 0))
    tr = lambda r: pl.BlockSpec((None, r, tm), lambda bi, i: (bi, 0, i))
    return pl.pallas_call(
        _inproj_kernel,
        grid=(b, s // tm),
        in_specs=[tok(d), pl.BlockSpec((None, N_MOD, d), _mod_index(mod)),
                  _full(nw.shape), _full(wz.shape), _full(wxm.shape), _full(wzt.shape), _full(wg.shape)],
        out_specs=[tok(2 * D_FOURIER), tok(D_MLSTM), tr(D_MLSTM), tr(GATE_ROWS)],
        out_shape=[jax.ShapeDtypeStruct((b, s, 2 * D_FOURIER), F32),
                   jax.ShapeDtypeStruct((b, s, D_MLSTM), F32),
                   jax.ShapeDtypeStruct((b, D_MLSTM, s), F32),
                   jax.ShapeDtypeStruct((b, GATE_ROWS, s), F32)],
        compiler_params=_cparams(2, 48),
        name="inproj",
    )(x, mod, nw, wz, wxm, wzt, wg)


def _convqkv_kernel(width, rows, halo, prev_ref, cur_ref, next_ref, cw_ref, cb_ref,
                    wqt_ref, wk_ref, wvt_ref, qt_ref, k_ref, vt_ref):
    i = pl.program_id(1)
    tm = cur_ref.shape[0]
    cur = cur_ref[...]
    prev = jnp.where(i > 0, prev_ref[...], 0.0)
    nxt = jnp.where(i < pl.num_programs(1) - 1, next_ref[...], 0.0)
    ext = jnp.concatenate([prev, cur, nxt], axis=0)
    n = ext.shape[0]
    r = lax.broadcasted_iota(jnp.int32, ext.shape, 0)
    col = (r + (width - halo % width)) & (width - 1)
    left = jnp.where(col == 0, 0.0, pltpu.roll(ext, 1, axis=0))
    right = jnp.where(col == width - 1, 0.0, pltpu.roll(ext, n - 1, axis=0))
    taps = (left, ext, right)
    cw = cw_ref[...]
    acc = jnp.zeros((tm, cur.shape[1]), F32) + cb_ref[...]
    for dy in (-1, 0, 1):
        if rows == 1 and dy != 0:
            continue
        lo = halo + dy * width
        for dx in (-1, 0, 1):
            k = (dy + 1) * 3 + (dx + 1)
            acc = acc + cw[k:k + 1] * taps[dx + 1][lo:lo + tm]
    cv = (acc * jax.nn.sigmoid(acc)).astype(BF16)
    xm = cur.astype(BF16)
    hd = MLSTM_HD
    for h in range(MLSTM_HEADS):
        sl = slice(h * hd, (h + 1) * hd)
        qt_ref[sl, :] = _dot_nt(wqt_ref[h], cv[:, sl]).astype(BF16)
        k_ref[:, sl] = (_dot(cv[:, sl], wk_ref[h]) * hd ** -0.5).astype(BF16)
        vt_ref[sl, :] = _dot_nt(wvt_ref[h], xm[:, sl]).astype(BF16)


def _convqkv(xm, width, cw, cb, wqt, wk, wvt):
    b, s, c = xm.shape
    rows = s // width
    tm = min(512, s)
    halo = LANES if rows > 1 else SUBLANES
    assert tm % width == 0 and tm % halo == 0 and (rows == 1 or halo > width)
    assert width & (width - 1) == 0
    per = tm // halo
    nblk = s // halo
    kern = functools.partial(_convqkv_kernel, width, rows, halo)
    tr = pl.BlockSpec((None, c, tm), lambda bi, i: (bi, 0, i))
    return pl.pallas_call(
        kern,
        grid=(b, s // tm),
        in_specs=[pl.BlockSpec((None, halo, c), lambda bi, i: (bi, jnp.maximum(i * per - 1, 0), 0)),
                  pl.BlockSpec((None, tm, c), lambda bi, i: (bi, i, 0)),
                  pl.BlockSpec((None, halo, c), lambda bi, i: (bi, jnp.minimum((i + 1) * per, nblk - 1), 0)),
                  _full(cw.shape), _full(cb.shape), _full(wqt.shape), _full(wk.shape), _full(wvt.shape)],
        out_specs=[tr, pl.BlockSpec((None, tm, c), lambda bi, i: (bi, i, 0)), tr],
        out_shape=[jax.ShapeDtypeStruct((b, c, s), BF16),
                   jax.ShapeDtypeStruct((b, s, c), BF16),
                   jax.ShapeDtypeStruct((b, c, s), BF16)],
        compiler_params=_cparams(2, 32),
        name="convqkv",
    )(xm, xm, xm, cw, cb, wqt, wk, wvt)


def _tri_ones():
    u = np.arange(CHUNK)[:, None]
    t = np.arange(CHUNK)[None, :]
    return jnp.asarray(np.concatenate([u <= t, u >= t], axis=0), BF16)


def _split_dot(x, w):
    hi = x.astype(BF16)
    r1 = x - hi.astype(F32)
    mid = r1.astype(BF16)
    lo = (r1 - mid.astype(F32)).astype(BF16)
    return _dot(hi, w) + _dot(mid, w) + _dot(lo, w)


def _mlstm_kernel(qtf_ref, kf_ref, vtf_ref, gtf_ref, qtb_ref, kb_ref, vtb_ref, gtb_ref,
                  bias_ref, tri_ref, c0_ref, m0_ref, htf_ref, htb_ref, cfin_ref, mfin_ref, c_st, m_st):
    j = pl.program_id(1)
    L = CHUNK
    hd = MLSTM_HD

    @pl.when(j == 0)
    def _():
        c_st[...] = c0_ref[...]
        m_st[...] = m0_ref[...]

    row_fwd = lax.broadcasted_iota(jnp.int32, (N_HD, L), 0) < MLSTM_HEADS
    src_row = lax.broadcasted_iota(jnp.int32, (L, L), 0)
    out_col = lax.broadcasted_iota(jnp.int32, (L, L), 1)
    one_row = jnp.where(lax.broadcasted_iota(jnp.int32, (PACKED_SUBLANES, L), 0) == 0, 1.0, 0.0).astype(BF16)
    bcast = lambda col: jnp.broadcast_to(col, (N_HD, L))

    def operands(u, p):
        d, hh = divmod(p, MLSTM_HEADS)
        hs = slice(hh * hd, (hh + 1) * hd)
        if d == 0:
            tl = slice(u * L, (u + 1) * L)
            return qtf_ref[hs, tl], kf_ref[tl, hs], vtf_ref[hs, tl], htf_ref, hs, tl
        tl = slice((SCAN_CHUNKS - 1 - u) * L, (SCAN_CHUNKS - u) * L)
        return qtb_ref[hs, tl], kb_ref[tl, hs], vtb_ref[hs, tl], htb_ref, hs, tl

    gates = []
    for u in range(SCAN_CHUNKS):
        fl = slice(u * L, (u + 1) * L)
        bl = slice((SCAN_CHUNKS - 1 - u) * L, (SCAN_CHUNKS - u) * L)
        ig = jnp.where(row_fwd, gtf_ref[:N_HD, fl], gtb_ref[:N_HD, bl]) + bias_ref[:N_HD]
        fp = jnp.where(row_fwd, gtf_ref[N_HD:, fl], gtb_ref[N_HD:, bl]) + bias_ref[N_HD:]
        lf = jax.nn.log_sigmoid(fp)
        lf2 = jnp.concatenate([jnp.where(row_fwd, lf, 0.0), jnp.where(row_fwd, 0.0, lf)], axis=1)
        bc = _split_dot(lf2, tri_ref[...])
        gs = ig - bc
        btot = jnp.where(row_fwd, bcast(bc[:, L - 1:L]), bcast(bc[:, 0:1]))
        gmax = bcast(jnp.max(gs, axis=1, keepdims=True))
        gates.append((bc, gs, btot, gmax, jnp.exp(gs - gmax)))

    scores, local = {}, {}
    for u in range(SCAN_CHUNKS):
        w_st = gates[u][4]
        for p in range(N_HD):
            qt, k, vt, _, _, _ = operands(u, p)
            scores[u, p] = _dot(k, qt)
            vaug = jnp.concatenate([vt, one_row], axis=0)
            vw = (vaug.astype(F32) * w_st[p:p + 1, :]).astype(BF16)
            local[u, p] = _dot(vw, k)

    for u in range(SCAN_CHUNKS):
        bc, gs, btot, gmax, _ = gates[u]
        m_prev = m_st[...]
        m_loc = btot + gmax
        m_new = jnp.maximum(btot + m_prev, m_loc)
        dec = jnp.exp(btot + m_prev - m_new)
        inj = jnp.exp(m_loc - m_new)
        m_st[...] = m_new
        for p in range(N_HD):
            qt, k, vt, dst, hs, tl = operands(u, p)
            seen = (src_row <= out_col) if p < MLSTM_HEADS else (src_row >= out_col)
            gs_seen = jnp.where(seen, jnp.broadcast_to(gs[p:p + 1, :], (L, L)).T, -jnp.inf)
            xm = jnp.maximum(m_prev[p:p + 1, :], jnp.max(gs_seen, axis=0, keepdims=True))
            pt = (scores[u, p] * jnp.exp(gs_seen - xm)).astype(BF16)
            qw = (qt.astype(F32) * jnp.exp(m_prev[p:p + 1, :] - xm)).astype(BF16)
            vaug = jnp.concatenate([vt, one_row], axis=0)
            c_prev = c_st[p]
            lhs = jnp.concatenate([vaug, c_prev.astype(BF16)], axis=1)
            rhs = jnp.concatenate([pt, qw], axis=0)
            o = _dot(lhs, rhs)
            den = jnp.maximum(jnp.abs(o[hd:hd + 1]), jnp.exp(-(bc[p:p + 1, :] + xm)))
            dst[hs, tl] = o[:hd] * (1.0 / den)
            c_st[p] = dec[p:p + 1, :] * c_prev + inj[p:p + 1, :] * local[u, p]

    @pl.when(j == pl.num_programs(1) - 1)
    def _():
        cfin_ref[...] = c_st[...]
        mfin_ref[...] = m_st[...]


def _mlstm(qt, k, vt, gt, bias, c0, m0):
    b, c, s = qt.shape
    tb = SCAN_CHUNKS * CHUNK
    nb = s // tb
    tri = _tri_ones()
    fwd_t = lambda bi, j: (bi, 0, j)
    bwd_t = lambda bi, j: (bi, 0, nb - 1 - j)
    fwd_n = lambda bi, j: (bi, j, 0)
    bwd_n = lambda bi, j: (bi, nb - 1 - j, 0)
    st_c = pl.BlockSpec((None, N_HD, STATE_ROWS, MLSTM_HD), lambda bi, j: (bi, 0, 0, 0))
    st_m = pl.BlockSpec((None, N_HD, CHUNK), lambda bi, j: (bi, 0, 0))
    side = lambda ti, ni: [pl.BlockSpec((None, c, tb), ti), pl.BlockSpec((None, tb, c), ni),
                           pl.BlockSpec((None, c, tb), ti), pl.BlockSpec((None, GATE_ROWS, tb), ti)]
    return pl.pallas_call(
        _mlstm_kernel,
        grid=(b, nb),
        in_specs=side(fwd_t, fwd_n) + side(bwd_t, bwd_n) + [_full(bias.shape), _full(tri.shape), st_c, st_m],
        out_specs=[pl.BlockSpec((None, c, tb), fwd_t), pl.BlockSpec((None, c, tb), bwd_t), st_c, st_m],
        out_shape=[jax.ShapeDtypeStruct((b, c, s), F32),
                   jax.ShapeDtypeStruct((b, c, s), F32),
                   jax.ShapeDtypeStruct(c0.shape, F32),
                   jax.ShapeDtypeStruct(m0.shape, F32)],
        scratch_shapes=[pltpu.VMEM((N_HD, STATE_ROWS, MLSTM_HD), F32),
                        pltpu.VMEM((N_HD, CHUNK), F32)],
        compiler_params=_cparams(2, 32),
        name="mlstm",
    )(qt, k, vt, gt, qt, k, vt, gt, bias, tri, c0, m0)


DFT_GROUP = SUBLANES


def _dft_tables(s, n2):
    n1 = s // n2
    k1 = np.arange(n1, dtype=np.float64)
    s1 = np.arange(n1, dtype=np.float64)
    ma = np.zeros((n2, 2 * n1, 2 * n1), np.float64)
    for s2 in range(n2):
        th = 2.0 * np.pi * np.outer(k1, s1 * n2 + s2) / s
        c, sn = np.cos(th), np.sin(th)
        ma[s2] = np.block([[c, sn], [-sn, c]])
    k2 = np.arange(n2, dtype=np.float64)
    ph = 2.0 * np.pi * np.outer(k2, k2) / n2
    eye = np.eye(DFT_GROUP)
    scale = 1.0 / math.sqrt(s)
    lb = np.einsum("ksr,ab->kasrb", np.stack([np.cos(ph), np.sin(ph)], axis=-1) * scale, eye)
    lb = lb.reshape(n2 * DFT_GROUP, n2 * 2 * DFT_GROUP)
    return jnp.asarray(ma, F32).astype(BF16), jnp.asarray(lb, F32).astype(BF16)


def _dft_a_kernel(z_ref, ma_ref, t_ref):
    for s in range(DFT_GROUP):
        zs = z_ref[:, s, :]
        rhs = jnp.concatenate([zs[:, :D_FOURIER], zs[:, D_FOURIER:]], axis=0).astype(BF16)
        t_ref[s] = _dot(ma_ref[s], rhs)


def _dft_b_kernel(t_ref, lb_ref, y_ref):
    n2 = t_ref.shape[0]
    rhs = t_ref[...].reshape(n2 * 2 * DFT_GROUP, D_FOURIER).astype(BF16)
    y_ref[...] = _dot(lb_ref[...], rhs).reshape(n2, DFT_GROUP, D_FOURIER)


def _seq_dft(z, n2=64):
    b, s, w = z.shape
    n1 = s // n2
    g = DFT_GROUP
    ma, lb = _dft_tables(s, n2)
    t = pl.pallas_call(
        _dft_a_kernel,
        grid=(b, n2 // g),
        in_specs=[pl.BlockSpec((None, n1, g, w), lambda bi, j: (bi, 0, j, 0)),
                  pl.BlockSpec((g, 2 * n1, 2 * n1), lambda bi, j: (j, 0, 0))],
        out_specs=pl.BlockSpec((None, g, 2 * n1, D_FOURIER), lambda bi, j: (bi, j, 0, 0)),
        out_shape=jax.ShapeDtypeStruct((b, n2, 2 * n1, D_FOURIER), F32),
        compiler_params=_cparams(2, 48),
        name="dft_stage1",
    )(z.reshape(b, n1, n2, w), ma)
    y = pl.pallas_call(
        _dft_b_kernel,
        grid=(b, n1 // g),
        in_specs=[pl.BlockSpec((None, n2, 2, g, D_FOURIER), lambda bi, q: (bi, 0, 0, q, 0)),
                  _full(lb.shape)],
        out_specs=pl.BlockSpec((None, n2, g, D_FOURIER), lambda bi, q: (bi, 0, q, 0)),
        out_shape=jax.ShapeDtypeStruct((b, n2, n1, D_FOURIER), F32),
        compiler_params=_cparams(2, 32),
        name="dft_stage2",
    )(t.reshape(b, n2, 2, n1, D_FOURIER), lb)
    return y.reshape(b, s, D_FOURIER)


def _dense_dft_kernel(z_ref, ld_ref, y_ref):
    rhs = jnp.concatenate([z_ref[:, :D_FOURIER], z_ref[:, D_FOURIER:]], axis=0).astype(BF16)
    y_ref[...] = _dot(ld_ref[...], rhs)


def _dense_dft(z):
    b, s, w = z.shape
    k = np.arange(s, dtype=np.float64)
    th = 2.0 * np.pi * np.outer(k, k) / s
    ld = jnp.asarray(np.concatenate([np.cos(th), np.sin(th)], axis=1) / math.sqrt(s), F32).astype(BF16)
    return pl.pallas_call(
        _dense_dft_kernel,
        grid=(b,),
        in_specs=[pl.BlockSpec((None, s, w), lambda bi: (bi, 0, 0)), _full(ld.shape)],
        out_specs=pl.BlockSpec((None, s, D_FOURIER), lambda bi: (bi, 0, 0)),
        out_shape=jax.ShapeDtypeStruct((b, s, D_FOURIER), F32),
        compiler_params=_cparams(1, 32),
        name="dft_dense",
    )(z, ld)


def _combine_kernel(x_ref, yf_ref, htf_ref, htb_ref, zt_ref, mod_ref, nw_ref, hnw_ref, wo_ref, o_ref):
    _, _, gate = _mod3(mod_ref[...], 1)
    hs = htf_ref[...] + htb_ref[...]
    hd = MLSTM_HD
    parts = []
    for h in range(MLSTM_HEADS):
        seg = hs[h * hd:(h + 1) * hd]
        mu = jnp.mean(seg, axis=0, keepdims=True)
        cen = seg - mu
        var = jnp.mean(cen * cen, axis=0, keepdims=True)
        parts.append(cen * lax.rsqrt(var + EPS))
    hn = jnp.concatenate(parts, axis=0) * hnw_ref[...]
    ym = (hn * jax.nn.sigmoid(zt_ref[...])).T.astype(BF16)
    y = _dot(yf_ref[...].astype(BF16), wo_ref[:D_FOURIER]) + _dot(ym, wo_ref[D_FOURIER:])
    o_ref[...] = x_ref[...] + gate * _rms(y, nw_ref[3:4])


def _combine(x, yf, htf, htb, zt, mod, nw, hnw, wo):
    b, s, d = x.shape
    tm = min(512, s)
    tok = lambda w: pl.BlockSpec((None, tm, w), lambda bi, i: (bi, i, 0))
    tr = pl.BlockSpec((None, D_MLSTM, tm), lambda bi, i: (bi, 0, i))
    hnw_b = jnp.broadcast_to(hnw.reshape(D_MLSTM, 1), (D_MLSTM, tm))
    return pl.pallas_call(
        _combine_kernel,
        grid=(b, s // tm),
        in_specs=[tok(d), tok(D_FOURIER), tr, tr, tr,
                  pl.BlockSpec((None, N_MOD, d), _mod_index(mod)),
                  _full(nw.shape), _full(hnw_b.shape), _full(wo.shape)],
        out_specs=tok(d),
        out_shape=jax.ShapeDtypeStruct(x.shape, F32),
        compiler_params=_cparams(2, 40),
        name="combine",
    )(x, yf, htf, htb, zt, mod, nw, hnw_b, wo)


def _gate_layout(w_gate, gate_b):
    h = MLSTM_HEADS
    d = w_gate.shape[0]
    pad_w = jnp.zeros((d, LANES - N_HD), w_gate.dtype)
    w = jnp.concatenate([w_gate[:, 0:h], w_gate[:, 2 * h:3 * h], pad_w,
                         w_gate[:, h:2 * h], w_gate[:, 3 * h:4 * h], pad_w], axis=1)
    bias = jnp.concatenate([gate_b[0], gate_b[2], gate_b[1], gate_b[3]])
    return w, jnp.broadcast_to(bias.reshape(GATE_ROWS, 1), (GATE_ROWS, CHUNK))


def kernel(x, c, ctx, c_ctx, w_ada, b_ada, norm_w, w_ff_in, w_ff_out, w_in, w_fmix,
           conv_w, conv_b, w_qkv, gate_b, mlstm_norm_w, w_out):
    b = x.shape[0]
    d = D_MODEL
    n_cond = b + 1
    assert n_cond <= SUBLANES
    cc = jnp.concatenate([c, c_ctx[None], jnp.zeros((SUBLANES - n_cond, d), F32)], axis=0)
    mod = _modulation(cc, w_ada, b_ada).reshape(DEPTH, SUBLANES, N_MOD, d)
    w_fold = _fold_fourier(w_in, w_fmix)

    xl, xc = x, ctx
    c_zero = jnp.zeros((b, N_HD, STATE_ROWS, MLSTM_HD), F32)
    m_zero = jnp.zeros((b, N_HD, CHUNK), F32)
    for l in range(DEPTH):
        last = l == DEPTH - 1
        mod_l = mod[l, :b]
        mod_c = mod[l, b:b + 1]
        nw = norm_w[l]
        ffw = [(w_ff_in[l, i, :, :D_FF].astype(BF16), w_ff_in[l, i, :, D_FF:].astype(BF16),
                w_ff_out[l, i].astype(BF16)) for i in range(2)]
        o_xm = D_FOURIER
        o_z = D_FOURIER + D_MLSTM
        o_g = D_FOURIER + 2 * D_MLSTM
        w_gate, g_bias = _gate_layout(w_in[l, :, o_g:], gate_b[l])
        pw = (w_fold[l], w_in[l, :, o_xm:o_z].astype(BF16), w_in[l, :, o_z:o_g].T.astype(BF16),
              w_gate.astype(BF16))
        cw = conv_w[l].reshape(9, D_MLSTM)
        cb = conv_b[l].reshape(1, D_MLSTM)
        qkvw = (jnp.swapaxes(w_qkv[l, 0], 1, 2).astype(BF16), w_qkv[l, 1].astype(BF16),
                jnp.swapaxes(w_qkv[l, 2], 1, 2).astype(BF16))
        hnw = mlstm_norm_w[l]
        wo = w_out[l].astype(BF16)

        xl = _ffn(xl, mod_l, 0, nw, *ffw[0])
        xc = _ffn(xc, mod_c, 0, nw, *ffw[0])
        z_l, xm_l, zt_l, gt_l = _inproj(xl, mod_l, nw, *pw)
        z_c, xm_c, zt_c, gt_c = _inproj(xc, mod_c, nw, *pw)
        qkv_l = _convqkv(xm_l, GRID_W, cw, cb, *qkvw)
        qkv_c = _convqkv(xm_c, xm_c.shape[1], cw, cb, *qkvw)
        htf_c, htb_c, c_ctx_st, m_ctx_st = _mlstm(*qkv_c, gt_c, g_bias, c_zero, m_zero)
        htf_l, htb_l, _, _ = _mlstm(*qkv_l, gt_l, g_bias, c_ctx_st, m_ctx_st)
        yf_l = _seq_dft(z_l)
        xl = _combine(xl, yf_l, htf_l, htb_l, zt_l, mod_l, nw, hnw, wo)
        xl = _ffn(xl, mod_l, 2, nw, *ffw[1])
        if not last:
            yf_c = _dense_dft(z_c)
            xc = _combine(xc, yf_c, htf_c, htb_c, zt_c, mod_c, nw, hnw, wo)
            xc = _ffn(xc, mod_c, 2, nw, *ffw[1])
    return xl
```

```python
import functools
import math

import numpy as np
import jax
import jax.numpy as jnp
from jax import lax
from jax.experimental import pallas as pl
from jax.experimental.pallas import tpu as pltpu

D_MODEL = 1024
DEPTH = 2
GRID_W = 64
D_FOURIER = 512
FOURIER_GROUPS = 8
FOURIER_GW = 64
D_MLSTM = 512
MLSTM_HEADS = 4
MLSTM_HD = 128
CHUNK = 128
D_FF = 2816
N_MOD = 9
FFN_RES = 0.5
EPS = 1e-6

LANES = 128
SUBLANES = 8
PACKED_SUBLANES = 16
MXU_DIM = 256
VMEM_BYTES = 64 * 1024 * 1024
N_HD = 2 * MLSTM_HEADS
STATE_ROWS = MLSTM_HD + PACKED_SUBLANES
GATE_ROWS = 2 * N_HD
SCAN_CHUNKS = 8

F32 = jnp.float32
BF16 = jnp.bfloat16
NT = (((1,), (1,)), ((), ()))


def _cparams(n_grid, vmem_mb):
    return pltpu.CompilerParams(
        dimension_semantics=("arbitrary",) * n_grid,
        vmem_limit_bytes=min(vmem_mb * 1024 * 1024, VMEM_BYTES - 4 * 1024 * 1024))


def _dot(a, b):
    return jnp.dot(a, b, preferred_element_type=F32)


def _dot_nt(a, b):
    return lax.dot_general(a, b, NT, preferred_element_type=F32)


def _rms(x, w):
    return x * lax.rsqrt(jnp.mean(x * x, axis=-1, keepdims=True) + EPS) * w


def _mod3(m, j):
    return m[3 * j:3 * j + 1], m[3 * j + 1:3 * j + 2], m[3 * j + 2:3 * j + 3]


def _mod_index(mod):
    if mod.shape[0] == 1:
        return lambda b, i: (0, 0, 0)
    return lambda b, i: (b, 0, 0)


def _full(shape):
    return pl.BlockSpec(shape, lambda *_: (0,) * len(shape))


def _mod_kernel(c_ref, w_ref, b_ref, o_ref):
    c = c_ref[...]
    s = (c * jax.nn.sigmoid(c)).astype(BF16)
    o_ref[...] = _dot(s, w_ref[...].astype(BF16)) + b_ref[...]


def _modulation(cc, w_ada, b_ada):
    d = D_MODEL
    return pl.pallas_call(
        _mod_kernel,
        grid=(DEPTH, N_MOD),
        in_specs=[pl.BlockSpec((SUBLANES, d), lambda l, n: (0, 0)),
                  pl.BlockSpec((None, d, d), lambda l, n: (l, 0, n)),
                  pl.BlockSpec((None, 1, d), lambda l, n: (l, 0, n))],
        out_specs=pl.BlockSpec((None, SUBLANES, d), lambda l, n: (l, 0, n)),
        out_shape=jax.ShapeDtypeStruct((DEPTH, SUBLANES, N_MOD * d), F32),
        compiler_params=_cparams(2, 32),
        name="modulation",
    )(cc, w_ada, b_ada.reshape(DEPTH, 1, N_MOD * d))


FF_CHUNK = MXU_DIM
N_FF_CHUNKS = D_FF // FF_CHUNK


def _ffn_kernel(j, x_ref, mod_ref, nw_ref, wg_ref, wu_ref, wo_ref, o_ref, a_ref):
    x = x_ref[...]
    shift, scale, gate = _mod3(mod_ref[...], j)
    h = (_rms(x, nw_ref[2 * j:2 * j + 1]) * (1.0 + scale) + shift).astype(BF16)
    for c in range(N_FF_CHUNKS):
        sl = slice(c * FF_CHUNK, (c + 1) * FF_CHUNK)
        g = _dot(h, wg_ref[:, sl])
        u = _dot(h, wu_ref[:, sl])
        a_ref[:, sl] = (g * jax.nn.sigmoid(g) * u).astype(BF16)
    y = _dot(a_ref[...], wo_ref[...])
    o_ref[...] = x + FFN_RES * gate * _rms(y, nw_ref[2 * j + 1:2 * j + 2])


def _ffn(x, mod, j, nw, wg, wu, wo):
    b, s, d = x.shape
    tm = min(512, s)
    kern = functools.partial(_ffn_kernel, j)
    tok = pl.BlockSpec((None, tm, d), lambda bi, i: (bi, i, 0))
    return pl.pallas_call(
        kern,
        grid=(b, s // tm),
        in_specs=[tok,
                  pl.BlockSpec((None, N_MOD, d), _mod_index(mod)),
                  _full(nw.shape), _full(wg.shape), _full(wu.shape), _full(wo.shape)],
        out_specs=tok,
        out_shape=jax.ShapeDtypeStruct(x.shape, F32),
        scratch_shapes=[pltpu.VMEM((tm, D_FF), BF16)],
        compiler_params=_cparams(2, 60),
        name="ffn",
    )(x, mod, nw, wg, wu, wo)


def _fold_kernel(wf_ref, bdw_ref, bdc_ref, bds_ref, o_ref):
    hi = lax.Precision.HIGHEST
    wf = wf_ref[...]
    bdw = bdw_ref[...]
    gr = jnp.dot(bdc_ref[...], bdw, precision=hi, preferred_element_type=F32)
    gi = jnp.dot(bds_ref[...], bdw, precision=hi, preferred_element_type=F32)
    o_ref[:, :D_FOURIER] = jnp.dot(wf, gr, precision=hi, preferred_element_type=F32).astype(BF16)
    o_ref[:, D_FOURIER:] = jnp.dot(wf, gi, precision=hi, preferred_element_type=F32).astype(BF16)


def _channel_dft_blocks():
    c = np.arange(FOURIER_GW, dtype=np.float64)
    ang = 2.0 * np.pi * np.outer(c, c) / FOURIER_GW
    eye = np.eye(FOURIER_GROUPS)
    scale = 1.0 / math.sqrt(FOURIER_GW)
    bdc = np.kron(eye, np.cos(ang) * scale)
    bds = np.kron(eye, -np.sin(ang) * scale)
    return jnp.asarray(bdc, F32), jnp.asarray(bds, F32)


def _fold_fourier(w_in, w_fmix):
    d = D_MODEL
    eye = jnp.eye(FOURIER_GROUPS, dtype=F32)
    bdw = jnp.einsum("gh,lgcd->lgchd", eye, w_fmix).reshape(DEPTH, D_FOURIER, D_FOURIER)
    bdc, bds = _channel_dft_blocks()
    sq = pl.BlockSpec((D_FOURIER, D_FOURIER), lambda l: (0, 0))
    return pl.pallas_call(
        _fold_kernel,
        grid=(DEPTH,),
        in_specs=[pl.BlockSpec((None, d, D_FOURIER), lambda l: (l, 0, 0)),
                  pl.BlockSpec((None, D_FOURIER, D_FOURIER), lambda l: (l, 0, 0)),
                  sq, sq],
        out_specs=pl.BlockSpec((None, d, 2 * D_FOURIER), lambda l: (l, 0, 0)),
        out_shape=jax.ShapeDtypeStruct((DEPTH, d, 2 * D_FOURIER), BF16),
        compiler_params=_cparams(1, 32),
        name="fold_fourier",
    )(w_in, bdw, bdc, bds)


def _inproj_kernel(x_ref, mod_ref, nw_ref, wz_ref, wxm_ref, wt_ref, z_ref, xm_ref, zt_ref, gt_ref):
    shift, scale, _ = _mod3(mod_ref[...], 1)
    h = (_rms(x_ref[...], nw_ref[2:3]) * (1.0 + scale) + shift).astype(BF16)
    z_ref[...] = _dot(h, wz_ref[...])
    xm_ref[...] = _dot(h, wxm_ref[...])
    t = _dot_nt(wt_ref[...], h)
    zt_ref[...] = t[:D_MLSTM]
    gt_ref[...] = t[D_MLSTM:]


def _inproj(x, mod, nw, wz, wxm, wt):
    b, s, d = x.shape
    tm = min(512, s)
    tok = lambda w: pl.BlockSpec((None, tm, w), lambda bi, i: (bi, i, 0))
    tr = lambda r: pl.BlockSpec((None, r, tm), lambda bi, i: (bi, 0, i))
    return pl.pallas_call(
        _inproj_kernel,
        grid=(b, s // tm),
        in_specs=[tok(d), pl.BlockSpec((None, N_MOD, d), _mod_index(mod)),
                  _full(nw.shape), _full(wz.shape), _full(wxm.shape), _full(wt.shape)],
        out_specs=[tok(2 * D_FOURIER), tok(D_MLSTM), tr(D_MLSTM), tr(GATE_ROWS)],
        out_shape=[jax.ShapeDtypeStruct((b, s, 2 * D_FOURIER), F32),
                   jax.ShapeDtypeStruct((b, s, D_MLSTM), F32),
                   jax.ShapeDtypeStruct((b, D_MLSTM, s), F32),
                   jax.ShapeDtypeStruct((b, GATE_ROWS, s), F32)],
        compiler_params=_cparams(2, 48),
        name="inproj",
    )(x, mod, nw, wz, wxm, wt)


def _convqkv_kernel(width, rows, halo, prev_ref, cur_ref, next_ref, cw_ref, cb_ref,
                    wqt_ref, wk_ref, wvt_ref, qt_ref, k_ref, vt_ref):
    i = pl.program_id(1)
    tm = cur_ref.shape[0]
    cur = cur_ref[...]
    prev = jnp.where(i > 0, prev_ref[...], 0.0)
    nxt = jnp.where(i < pl.num_programs(1) - 1, next_ref[...], 0.0)
    ext = jnp.concatenate([prev, cur, nxt], axis=0)
    n = ext.shape[0]
    r = lax.broadcasted_iota(jnp.int32, ext.shape, 0)
    col = (r + (width - halo % width)) & (width - 1)
    left = jnp.where(col == 0, 0.0, pltpu.roll(ext, 1, axis=0))
    right = jnp.where(col == width - 1, 0.0, pltpu.roll(ext, n - 1, axis=0))
    taps = (left, ext, right)
    cw = cw_ref[...]
    acc = jnp.zeros((tm, cur.shape[1]), F32) + cb_ref[...]
    for dy in (-1, 0, 1):
        if rows == 1 and dy != 0:
            continue
        lo = halo + dy * width
        for dx in (-1, 0, 1):
            k = (dy + 1) * 3 + (dx + 1)
            acc = acc + cw[k:k + 1] * taps[dx + 1][lo:lo + tm]
    cv = (acc * jax.nn.sigmoid(acc)).astype(BF16)
    xm = cur.astype(BF16)
    hd = MLSTM_HD
    for h in range(MLSTM_HEADS):
        sl = slice(h * hd, (h + 1) * hd)
        qt_ref[sl, :] = _dot_nt(wqt_ref[h], cv[:, sl]).astype(BF16)
        k_ref[:, sl] = (_dot(cv[:, sl], wk_ref[h]) * hd ** -0.5).astype(BF16)
        vt_ref[sl, :] = _dot_nt(wvt_ref[h], xm[:, sl]).astype(BF16)


def _convqkv(xm, width, cw, cb, wqt, wk, wvt):
    b, s, c = xm.shape
    rows = s // width
    tm = min(512, s)
    halo = LANES if rows > 1 else SUBLANES
    assert tm % width == 0 and tm % halo == 0 and (rows == 1 or halo > width)
    assert width & (width - 1) == 0
    per = tm // halo
    nblk = s // halo
    kern = functools.partial(_convqkv_kernel, width, rows, halo)
    tr = pl.BlockSpec((None, c, tm), lambda bi, i: (bi, 0, i))
    return pl.pallas_call(
        kern,
        grid=(b, s // tm),
        in_specs=[pl.BlockSpec((None, halo, c), lambda bi, i: (bi, jnp.maximum(i * per - 1, 0), 0)),
                  pl.BlockSpec((None, tm, c), lambda bi, i: (bi, i, 0)),
                  pl.BlockSpec((None, halo, c), lambda bi, i: (bi, jnp.minimum((i + 1) * per, nblk - 1), 0)),
                  _full(cw.shape), _full(cb.shape), _full(wqt.shape), _full(wk.shape), _full(wvt.shape)],
        out_specs=[tr, pl.BlockSpec((None, tm, c), lambda bi, i: (bi, i, 0)), tr],
        out_shape=[jax.ShapeDtypeStruct((b, c, s), BF16),
                   jax.ShapeDtypeStruct((b, s, c), BF16),
                   jax.ShapeDtypeStruct((b, c, s), BF16)],
        compiler_params=_cparams(2, 32),
        name="convqkv",
    )(xm, xm, xm, cw, cb, wqt, wk, wvt)


def _tri_ones():
    u = np.arange(CHUNK)[:, None]
    t = np.arange(CHUNK)[None, :]
    return jnp.asarray(np.concatenate([u <= t, u >= t], axis=0), BF16)


def _split_dot(x, w):
    hi = x.astype(BF16)
    r1 = x - hi.astype(F32)
    mid = r1.astype(BF16)
    lo = (r1 - mid.astype(F32)).astype(BF16)
    return _dot(hi, w) + _dot(mid, w) + _dot(lo, w)


def _mlstm_kernel(qtf_ref, kf_ref, vtf_ref, gtf_ref, qtb_ref, kb_ref, vtb_ref, gtb_ref,
                  bias_ref, tri_ref, c0_ref, m0_ref, htf_ref, htb_ref, cfin_ref, mfin_ref, c_st, m_st):
    j = pl.program_id(1)
    L = CHUNK
    hd = MLSTM_HD
    n_sub = gtf_ref.shape[1] // L

    @pl.when(j == 0)
    def _():
        c_st[...] = c0_ref[...]
        m_st[...] = m0_ref[...]

    row_fwd = lax.broadcasted_iota(jnp.int32, (N_HD, L), 0) < MLSTM_HEADS
    src_row = lax.broadcasted_iota(jnp.int32, (L, L), 0)
    out_col = lax.broadcasted_iota(jnp.int32, (L, L), 1)
    one_row = jnp.where(lax.broadcasted_iota(jnp.int32, (PACKED_SUBLANES, L), 0) == 0, 1.0, 0.0).astype(BF16)
    bcast = lambda col: jnp.broadcast_to(col, (N_HD, L))

    def operands(u, p):
        d, hh = divmod(p, MLSTM_HEADS)
        hs = slice(hh * hd, (hh + 1) * hd)
        if d == 0:
            tl = slice(u * L, (u + 1) * L)
            return qtf_ref[hs, tl], kf_ref[tl, hs], vtf_ref[hs, tl], htf_ref, hs, tl
        tl = slice((n_sub - 1 - u) * L, (n_sub - u) * L)
        return qtb_ref[hs, tl], kb_ref[tl, hs], vtb_ref[hs, tl], htb_ref, hs, tl

    gates = []
    for u in range(n_sub):
        fl = slice(u * L, (u + 1) * L)
        bl = slice((n_sub - 1 - u) * L, (n_sub - u) * L)
        ig = jnp.where(row_fwd, gtf_ref[:N_HD, fl], gtb_ref[:N_HD, bl]) + bias_ref[:N_HD]
        fp = jnp.where(row_fwd, gtf_ref[N_HD:, fl], gtb_ref[N_HD:, bl]) + bias_ref[N_HD:]
        lf = jax.nn.log_sigmoid(fp)
        lf2 = jnp.concatenate([jnp.where(row_fwd, lf, 0.0), jnp.where(row_fwd, 0.0, lf)], axis=1)
        bc = _split_dot(lf2, tri_ref[...])
        gs = ig - bc
        btot = jnp.where(row_fwd, bcast(bc[:, L - 1:L]), bcast(bc[:, 0:1]))
        gmax = bcast(jnp.max(gs, axis=1, keepdims=True))
        gates.append((bc, gs, btot, gmax, jnp.exp(gs - gmax)))

    scores, local = {}, {}
    for u in range(n_sub):
        w_st = gates[u][4]
        for p in range(N_HD):
            qt, k, vt, _, _, _ = operands(u, p)
            scores[u, p] = _dot(k, qt)
            vaug = jnp.concatenate([vt, one_row], axis=0)
            vw = (vaug.astype(F32) * w_st[p:p + 1, :]).astype(BF16)
            local[u, p] = _dot(vw, k)

    for u in range(n_sub):
        bc, gs, btot, gmax, _ = gates[u]
        m_prev = m_st[...]
        m_loc = btot + gmax
        m_new = jnp.maximum(btot + m_prev, m_loc)
        dec = jnp.exp(btot + m_prev - m_new)
        inj = jnp.exp(m_loc - m_new)
        m_st[...] = m_new
        for p in range(N_HD):
            qt, k, vt, dst, hs, tl = operands(u, p)
            seen = (src_row <= out_col) if p < MLSTM_HEADS else (src_row >= out_col)
            gs_seen = jnp.where(seen, jnp.broadcast_to(gs[p:p + 1, :], (L, L)).T, -jnp.inf)
            xm = jnp.maximum(m_prev[p:p + 1, :], jnp.max(gs_seen, axis=0, keepdims=True))
            pt = (scores[u, p] * jnp.exp(gs_seen - xm)).astype(BF16)
            qw = (qt.astype(F32) * jnp.exp(m_prev[p:p + 1, :] - xm)).astype(BF16)
            vaug = jnp.concatenate([vt, one_row], axis=0)
            c_prev = c_st[p]
            lhs = jnp.concatenate([vaug, c_prev.astype(BF16)], axis=1)
            rhs = jnp.concatenate([pt, qw], axis=0)
            o = _dot(lhs, rhs)
            den = jnp.maximum(jnp.abs(o[hd:hd + 1]), jnp.exp(-(bc[p:p + 1, :] + xm)))
            dst[hs, tl] = o[:hd] * (1.0 / den)
            c_st[p] = dec[p:p + 1, :] * c_prev + inj[p:p + 1, :] * local[u, p]

    @pl.when(j == pl.num_programs(1) - 1)
    def _():
        cfin_ref[...] = c_st[...]
        mfin_ref[...] = m_st[...]


def _mlstm(qt, k, vt, gt, bias, c0, m0):
    b, c, s = qt.shape
    tb = min(SCAN_CHUNKS * CHUNK, s)
    nb = s // tb
    tri = _tri_ones()
    fwd_t = lambda bi, j: (bi, 0, j)
    bwd_t = lambda bi, j: (bi, 0, nb - 1 - j)
    fwd_n = lambda bi, j: (bi, j, 0)
    bwd_n = lambda bi, j: (bi, nb - 1 - j, 0)
    st_c = pl.BlockSpec((None, N_HD, STATE_ROWS, MLSTM_HD), lambda bi, j: (bi, 0, 0, 0))
    st_m = pl.BlockSpec((None, N_HD, CHUNK), lambda bi, j: (bi, 0, 0))
    side = lambda ti, ni: [pl.BlockSpec((None, c, tb), ti), pl.BlockSpec((None, tb, c), ni),
                           pl.BlockSpec((None, c, tb), ti), pl.BlockSpec((None, GATE_ROWS, tb), ti)]
    return pl.pallas_call(
        _mlstm_kernel,
        grid=(b, nb),
        in_specs=side(fwd_t, fwd_n) + side(bwd_t, bwd_n) + [_full(bias.shape), _full(tri.shape), st_c, st_m],
        out_specs=[pl.BlockSpec((None, c, tb), fwd_t), pl.BlockSpec((None, c, tb), bwd_t), st_c, st_m],
        out_shape=[jax.ShapeDtypeStruct((b, c, s), F32),
                   jax.ShapeDtypeStruct((b, c, s), F32),
                   jax.ShapeDtypeStruct(c0.shape, F32),
                   jax.ShapeDtypeStruct(m0.shape, F32)],
        scratch_shapes=[pltpu.VMEM((N_HD, STATE_ROWS, MLSTM_HD), F32),
                        pltpu.VMEM((N_HD, CHUNK), F32)],
        compiler_params=_cparams(2, 56),
        name="mlstm",
    )(qt, k, vt, gt, qt, k, vt, gt, bias, tri, c0, m0)


DFT_GROUP = SUBLANES


def _dft_tables(s, n2):
    n1 = s // n2
    k1 = np.arange(n1, dtype=np.float64)
    s1 = np.arange(n1, dtype=np.float64)
    ma = np.zeros((n2, 2 * n1, 2 * n1), np.float64)
    for s2 in range(n2):
        th = 2.0 * np.pi * np.outer(k1, s1 * n2 + s2) / s
        c, sn = np.cos(th), np.sin(th)
        ma[s2] = np.block([[c, sn], [-sn, c]])
    k2 = np.arange(n2, dtype=np.float64)
    ph = 2.0 * np.pi * np.outer(k2, k2) / n2
    eye = np.eye(DFT_GROUP)
    scale = 1.0 / math.sqrt(s)
    lb = np.einsum("ksr,ab->kasrb", np.stack([np.cos(ph), np.sin(ph)], axis=-1) * scale, eye)
    lb = lb.reshape(n2 * DFT_GROUP, n2 * 2 * DFT_GROUP)
    return jnp.asarray(ma, F32).astype(BF16), jnp.asarray(lb, F32).astype(BF16)


def _seq_dft_kernel(n_a, z_ref, ma_ref, lb_ref, y_ref, t_scr):
    j = pl.program_id(1)
    g = DFT_GROUP
    n2 = t_scr.shape[0]
    n1 = t_scr.shape[2]

    @pl.when(j < n_a)
    def _():
        for s in range(g):
            zs = z_ref[:, s, :]
            rhs = jnp.concatenate([zs[:, :D_FOURIER], zs[:, D_FOURIER:]], axis=0).astype(BF16)
            t_scr[j * g + s] = _dot(ma_ref[s], rhs).reshape(2, n1, D_FOURIER)

    @pl.when(j >= n_a)
    def _():
        k0 = pl.multiple_of((j - n_a) * g, g)
        rhs = t_scr[:, :, pl.ds(k0, g), :].reshape(n2 * 2 * g, D_FOURIER).astype(BF16)
        y_ref[...] = _dot(lb_ref[...], rhs).reshape(n2, g, D_FOURIER)


def _seq_dft(z, n2=64):
    b, s, w = z.shape
    n1 = s // n2
    g = DFT_GROUP
    n_a = n2 // g
    n_b = n1 // g
    ma, lb = _dft_tables(s, n2)
    y = pl.pallas_call(
        functools.partial(_seq_dft_kernel, n_a),
        grid=(b, n_a + n_b),
        in_specs=[pl.BlockSpec((None, n1, g, w), lambda bi, j: (bi, 0, jnp.minimum(j, n_a - 1), 0)),
                  pl.BlockSpec((g, 2 * n1, 2 * n1), lambda bi, j: (jnp.minimum(j, n_a - 1), 0, 0)),
                  _full(lb.shape)],
        out_specs=pl.BlockSpec((None, n2, g, D_FOURIER), lambda bi, j: (bi, 0, jnp.maximum(j - n_a, 0), 0)),
        out_shape=jax.ShapeDtypeStruct((b, n2, n1, D_FOURIER), F32),
        scratch_shapes=[pltpu.VMEM((n2, 2, n1, D_FOURIER), F32)],
        compiler_params=_cparams(2, 60),
        name="seq_dft",
    )(z.reshape(b, n1, n2, w), ma, lb)
    return y.reshape(b, s, D_FOURIER)


def _dense_dft_kernel(z_ref, ld_ref, y_ref):
    rhs = jnp.concatenate([z_ref[:, :D_FOURIER], z_ref[:, D_FOURIER:]], axis=0).astype(BF16)
    y_ref[...] = _dot(ld_ref[...], rhs)


def _dense_dft(z):
    b, s, w = z.shape
    k = np.arange(s, dtype=np.float64)
    th = 2.0 * np.pi * np.outer(k, k) / s
    ld = jnp.asarray(np.concatenate([np.cos(th), np.sin(th)], axis=1) / math.sqrt(s), F32).astype(BF16)
    return pl.pallas_call(
        _dense_dft_kernel,
        grid=(b,),
        in_specs=[pl.BlockSpec((None, s, w), lambda bi: (bi, 0, 0)), _full(ld.shape)],
        out_specs=pl.BlockSpec((None, s, D_FOURIER), lambda bi: (bi, 0, 0)),
        out_shape=jax.ShapeDtypeStruct((b, s, D_FOURIER), F32),
        compiler_params=_cparams(1, 32),
        name="dft_dense",
    )(z, ld)


def _combine_kernel(x_ref, yf_ref, htf_ref, htb_ref, zt_ref, mod_ref, nw_ref, hnw_ref, wo_ref, o_ref):
    _, _, gate = _mod3(mod_ref[...], 1)
    hs = htf_ref[...] + htb_ref[...]
    hd = MLSTM_HD
    parts = []
    for h in range(MLSTM_HEADS):
        seg = hs[h * hd:(h + 1) * hd]
        mu = jnp.mean(seg, axis=0, keepdims=True)
        cen = seg - mu
        var = jnp.mean(cen * cen, axis=0, keepdims=True)
        parts.append(cen * lax.rsqrt(var + EPS))
    hn = jnp.concatenate(parts, axis=0) * hnw_ref[...]
    ym = (hn * jax.nn.sigmoid(zt_ref[...])).T.astype(BF16)
    y = _dot(yf_ref[...].astype(BF16), wo_ref[:D_FOURIER]) + _dot(ym, wo_ref[D_FOURIER:])
    o_ref[...] = x_ref[...] + gate * _rms(y, nw_ref[3:4])


def _combine(x, yf, htf, htb, zt, mod, nw, hnw, wo):
    b, s, d = x.shape
    tm = min(512, s)
    tok = lambda w: pl.BlockSpec((None, tm, w), lambda bi, i: (bi, i, 0))
    tr = pl.BlockSpec((None, D_MLSTM, tm), lambda bi, i: (bi, 0, i))
    hnw_b = jnp.broadcast_to(hnw.reshape(D_MLSTM, 1), (D_MLSTM, tm))
    return pl.pallas_call(
        _combine_kernel,
        grid=(b, s // tm),
        in_specs=[tok(d), tok(D_FOURIER), tr, tr, tr,
                  pl.BlockSpec((None, N_MOD, d), _mod_index(mod)),
                  _full(nw.shape), _full(hnw_b.shape), _full(wo.shape)],
        out_specs=tok(d),
        out_shape=jax.ShapeDtypeStruct(x.shape, F32),
        compiler_params=_cparams(2, 40),
        name="combine",
    )(x, yf, htf, htb, zt, mod, nw, hnw_b, wo)


def _gate_layout(w_gate, gate_b):
    h = MLSTM_HEADS
    wt = w_gate.T
    w = jnp.concatenate([wt[0:h], wt[2 * h:3 * h], wt[h:2 * h], wt[3 * h:4 * h]], axis=0)
    bias = jnp.concatenate([gate_b[0], gate_b[2], gate_b[1], gate_b[3]])
    return w, jnp.broadcast_to(bias.reshape(GATE_ROWS, 1), (GATE_ROWS, CHUNK))


def kernel(x, c, ctx, c_ctx, w_ada, b_ada, norm_w, w_ff_in, w_ff_out, w_in, w_fmix,
           conv_w, conv_b, w_qkv, gate_b, mlstm_norm_w, w_out):
    b = x.shape[0]
    d = D_MODEL
    n_cond = b + 1
    assert n_cond <= SUBLANES
    cc = jnp.concatenate([c, c_ctx[None], jnp.zeros((SUBLANES - n_cond, d), F32)], axis=0)
    mod = _modulation(cc, w_ada, b_ada).reshape(DEPTH, SUBLANES, N_MOD, d)
    w_fold = _fold_fourier(w_in, w_fmix)

    xl, xc = x, ctx
    c_zero = jnp.zeros((b, N_HD, STATE_ROWS, MLSTM_HD), F32)
    m_zero = jnp.zeros((b, N_HD, CHUNK), F32)
    for l in range(DEPTH):
        last = l == DEPTH - 1
        mod_l = mod[l, :b]
        mod_c = mod[l, b:b + 1]
        nw = norm_w[l]
        ffw = [(w_ff_in[l, i, :, :D_FF].astype(BF16), w_ff_in[l, i, :, D_FF:].astype(BF16),
                w_ff_out[l, i].astype(BF16)) for i in range(2)]
        o_xm = D_FOURIER
        o_z = D_FOURIER + D_MLSTM
        o_g = D_FOURIER + 2 * D_MLSTM
        w_gate, g_bias = _gate_layout(w_in[l, :, o_g:], gate_b[l])
        pw = (w_fold[l], w_in[l, :, o_xm:o_z].astype(BF16),
              jnp.concatenate([w_in[l, :, o_z:o_g].T, w_gate], axis=0).astype(BF16))
        cw = conv_w[l].reshape(9, D_MLSTM)
        cb = conv_b[l].reshape(1, D_MLSTM)
        qkvw = (jnp.swapaxes(w_qkv[l, 0], 1, 2).astype(BF16), w_qkv[l, 1].astype(BF16),
                jnp.swapaxes(w_qkv[l, 2], 1, 2).astype(BF16))
        hnw = mlstm_norm_w[l]
        wo = w_out[l].astype(BF16)

        xl = _ffn(xl, mod_l, 0, nw, *ffw[0])
        xc = _ffn(xc, mod_c, 0, nw, *ffw[0])
        z_l, xm_l, zt_l, gt_l = _inproj(xl, mod_l, nw, *pw)
        z_c, xm_c, zt_c, gt_c = _inproj(xc, mod_c, nw, *pw)
        qkv_l = _convqkv(xm_l, GRID_W, cw, cb, *qkvw)
        qkv_c = _convqkv(xm_c, xm_c.shape[1], cw, cb, *qkvw)
        htf_c, htb_c, c_ctx_st, m_ctx_st = _mlstm(*qkv_c, gt_c, g_bias, c_zero, m_zero)
        htf_l, htb_l, _, _ = _mlstm(*qkv_l, gt_l, g_bias, c_ctx_st, m_ctx_st)
        yf_l = _seq_dft(z_l)
        xl = _combine(xl, yf_l, htf_l, htb_l, zt_l, mod_l, nw, hnw, wo)
        xl = _ffn(xl, mod_l, 2, nw, *ffw[1])
        if not last:
            yf_c = _dense_dft(z_c)
            xc = _combine(xc, yf_c, htf_c, htb_c, zt_c, mod_c, nw, hnw, wo)
            xc = _ffn(xc, mod_c, 2, nw, *ffw[1])
    return xl
```

```python
import functools
import math

import numpy as np
import jax
import jax.numpy as jnp
from jax import lax
from jax.experimental import pallas as pl
from jax.experimental.pallas import tpu as pltpu

D_MODEL = 1024
DEPTH = 2
GRID_W = 64
D_FOURIER = 512
FOURIER_GROUPS = 8
FOURIER_GW = 64
D_MLSTM = 512
MLSTM_HEADS = 4
MLSTM_HD = 128
CHUNK = 128
D_FF = 2816
N_MOD = 9
FFN_RES = 0.5
EPS = 1e-6

LANES = 128
SUBLANES = 8
PACKED_SUBLANES = 16
MXU_DIM = 256
VMEM_BYTES = 64 * 1024 * 1024
N_HD = 2 * MLSTM_HEADS
STATE_ROWS = MLSTM_HD + PACKED_SUBLANES
GATE_ROWS = 2 * N_HD
SCAN_CHUNKS = 8

F32 = jnp.float32
BF16 = jnp.bfloat16
NT = (((1,), (1,)), ((), ()))


def _cparams(n_grid, vmem_mb):
    return pltpu.CompilerParams(
        dimension_semantics=("arbitrary",) * n_grid,
        vmem_limit_bytes=min(vmem_mb * 1024 * 1024, VMEM_BYTES - 4 * 1024 * 1024))


def _dot(a, b):
    return jnp.dot(a, b, preferred_element_type=F32)


def _dot_nt(a, b):
    return lax.dot_general(a, b, NT, preferred_element_type=F32)


def _rms(x, w):
    return x * lax.rsqrt(jnp.mean(x * x, axis=-1, keepdims=True) + EPS) * w


def _mod3(m, j):
    return m[3 * j:3 * j + 1], m[3 * j + 1:3 * j + 2], m[3 * j + 2:3 * j + 3]


def _mod_index(mod):
    if mod.shape[0] == 1:
        return lambda b, i: (0, 0, 0)
    return lambda b, i: (b, 0, 0)


def _full(shape):
    return pl.BlockSpec(shape, lambda *_: (0,) * len(shape))


def _resident(shape):
    return pl.BlockSpec(shape, lambda *_: (0,) * len(shape), pipeline_mode=pl.Buffered(1))


def _mod_kernel(c_ref, w_ref, b_ref, o_ref):
    c = c_ref[...]
    s = (c * jax.nn.sigmoid(c)).astype(BF16)
    o_ref[...] = _dot(s, w_ref[...].astype(BF16)) + b_ref[...]


def _modulation(cc, w_ada, b_ada):
    d = D_MODEL
    return pl.pallas_call(
        _mod_kernel,
        grid=(DEPTH, N_MOD),
        in_specs=[pl.BlockSpec((SUBLANES, d), lambda l, n: (0, 0)),
                  pl.BlockSpec((None, d, d), lambda l, n: (l, 0, n)),
                  pl.BlockSpec((None, 1, d), lambda l, n: (l, 0, n))],
        out_specs=pl.BlockSpec((None, SUBLANES, d), lambda l, n: (l, 0, n)),
        out_shape=jax.ShapeDtypeStruct((DEPTH, SUBLANES, N_MOD * d), F32),
        compiler_params=_cparams(2, 32),
        name="modulation",
    )(cc, w_ada, b_ada.reshape(DEPTH, 1, N_MOD * d))


FF_CHUNK = MXU_DIM
N_FF_CHUNKS = D_FF // FF_CHUNK


def _ffn_body(x, mod, j, nw_ref, wg_ref, wu_ref, wo_ref, a_ref):
    shift, scale, gate = _mod3(mod, j)
    h = (_rms(x, nw_ref[2 * j:2 * j + 1]) * (1.0 + scale) + shift).astype(BF16)
    for c in range(N_FF_CHUNKS):
        sl = slice(c * FF_CHUNK, (c + 1) * FF_CHUNK)
        g = _dot(h, wg_ref[:, sl])
        u = _dot(h, wu_ref[:, sl])
        a_ref[:, sl] = (g * jax.nn.sigmoid(g) * u).astype(BF16)
    y = _dot(a_ref[...], wo_ref[...])
    return x + FFN_RES * gate * _rms(y, nw_ref[2 * j + 1:2 * j + 2])


def _fold_kernel(wf_ref, bdw_ref, bdc_ref, bds_ref, o_ref):
    hi = lax.Precision.HIGHEST
    wf = wf_ref[...]
    bdw = bdw_ref[...]
    gr = jnp.dot(bdc_ref[...], bdw, precision=hi, preferred_element_type=F32)
    gi = jnp.dot(bds_ref[...], bdw, precision=hi, preferred_element_type=F32)
    o_ref[:, :D_FOURIER] = jnp.dot(wf, gr, precision=hi, preferred_element_type=F32).astype(BF16)
    o_ref[:, D_FOURIER:] = jnp.dot(wf, gi, precision=hi, preferred_element_type=F32).astype(BF16)


def _channel_dft_blocks():
    c = np.arange(FOURIER_GW, dtype=np.float64)
    ang = 2.0 * np.pi * np.outer(c, c) / FOURIER_GW
    eye = np.eye(FOURIER_GROUPS)
    scale = 1.0 / math.sqrt(FOURIER_GW)
    bdc = np.kron(eye, np.cos(ang) * scale)
    bds = np.kron(eye, -np.sin(ang) * scale)
    return jnp.asarray(bdc, F32), jnp.asarray(bds, F32)


def _fold_fourier(w_in, w_fmix):
    d = D_MODEL
    eye = jnp.eye(FOURIER_GROUPS, dtype=F32)
    bdw = jnp.einsum("gh,lgcd->lgchd", eye, w_fmix).reshape(DEPTH, D_FOURIER, D_FOURIER)
    bdc, bds = _channel_dft_blocks()
    sq = pl.BlockSpec((D_FOURIER, D_FOURIER), lambda l: (0, 0))
    return pl.pallas_call(
        _fold_kernel,
        grid=(DEPTH,),
        in_specs=[pl.BlockSpec((None, d, D_FOURIER), lambda l: (l, 0, 0)),
                  pl.BlockSpec((None, D_FOURIER, D_FOURIER), lambda l: (l, 0, 0)),
                  sq, sq],
        out_specs=pl.BlockSpec((None, d, 2 * D_FOURIER), lambda l: (l, 0, 0)),
        out_shape=jax.ShapeDtypeStruct((DEPTH, d, 2 * D_FOURIER), BF16),
        compiler_params=_cparams(1, 32),
        name="fold_fourier",
    )(w_in, bdw, bdc, bds)


def _ffn_inproj_kernel(x_ref, mod_ref, nw_ref, wg_ref, wu_ref, wo_ref, wz_ref, wxm_ref, wt_ref,
                       o_ref, z_ref, xm_ref, zt_ref, gt_ref, a_ref):
    mod = mod_ref[...]
    x1 = _ffn_body(x_ref[...], mod, 0, nw_ref, wg_ref, wu_ref, wo_ref, a_ref)
    o_ref[...] = x1
    shift, scale, _ = _mod3(mod, 1)
    h = (_rms(x1, nw_ref[2:3]) * (1.0 + scale) + shift).astype(BF16)
    z_ref[...] = _dot(h, wz_ref[...])
    xm_ref[...] = _dot(h, wxm_ref[...])
    t = _dot_nt(wt_ref[...], h)
    zt_ref[...] = t[:D_MLSTM]
    gt_ref[...] = t[D_MLSTM:]


def _ffn_inproj(x, mod, nw, wg, wu, wo, wz, wxm, wt):
    b, s, d = x.shape
    tm = min(512, s)
    tok = lambda w: pl.BlockSpec((None, tm, w), lambda bi, i: (bi, i, 0))
    tr = lambda r: pl.BlockSpec((None, r, tm), lambda bi, i: (bi, 0, i))
    return pl.pallas_call(
        _ffn_inproj_kernel,
        grid=(b, s // tm),
        in_specs=[tok(d), pl.BlockSpec((None, N_MOD, d), _mod_index(mod)), _resident(nw.shape),
                  _resident(wg.shape), _resident(wu.shape), _resident(wo.shape),
                  _resident(wz.shape), _resident(wxm.shape), _resident(wt.shape)],
        out_specs=[tok(d), tok(2 * D_FOURIER), tok(D_MLSTM), tr(D_MLSTM), tr(GATE_ROWS)],
        out_shape=[jax.ShapeDtypeStruct(x.shape, F32),
                   jax.ShapeDtypeStruct((b, s, 2 * D_FOURIER), F32),
                   jax.ShapeDtypeStruct((b, s, D_MLSTM), F32),
                   jax.ShapeDtypeStruct((b, D_MLSTM, s), F32),
                   jax.ShapeDtypeStruct((b, GATE_ROWS, s), F32)],
        scratch_shapes=[pltpu.VMEM((tm, D_FF), BF16)],
        compiler_params=_cparams(2, 60),
        name="ffn_inproj",
    )(x, mod, nw, wg, wu, wo, wz, wxm, wt)


def _convqkv_kernel(width, rows, halo, prev_ref, cur_ref, next_ref, cw_ref, cb_ref,
                    wqt_ref, wk_ref, wvt_ref, qt_ref, k_ref, vt_ref):
    i = pl.program_id(1)
    tm = cur_ref.shape[0]
    cur = cur_ref[...]
    prev = jnp.where(i > 0, prev_ref[...], 0.0)
    nxt = jnp.where(i < pl.num_programs(1) - 1, next_ref[...], 0.0)
    ext = jnp.concatenate([prev, cur, nxt], axis=0)
    n = ext.shape[0]
    r = lax.broadcasted_iota(jnp.int32, ext.shape, 0)
    col = (r + (width - halo % width)) & (width - 1)
    left = jnp.where(col == 0, 0.0, pltpu.roll(ext, 1, axis=0))
    right = jnp.where(col == width - 1, 0.0, pltpu.roll(ext, n - 1, axis=0))
    taps = (left, ext, right)
    cw = cw_ref[...]
    acc = jnp.zeros((tm, cur.shape[1]), F32) + cb_ref[...]
    for dy in (-1, 0, 1):
        if rows == 1 and dy != 0:
            continue
        lo = halo + dy * width
        for dx in (-1, 0, 1):
            k = (dy + 1) * 3 + (dx + 1)
            acc = acc + cw[k:k + 1] * taps[dx + 1][lo:lo + tm]
    cv = (acc * jax.nn.sigmoid(acc)).astype(BF16)
    xm = cur.astype(BF16)
    hd = MLSTM_HD
    for h in range(MLSTM_HEADS):
        sl = slice(h * hd, (h + 1) * hd)
        qt_ref[sl, :] = _dot_nt(wqt_ref[h], cv[:, sl]).astype(BF16)
        k_ref[:, sl] = (_dot(cv[:, sl], wk_ref[h]) * hd ** -0.5).astype(BF16)
        vt_ref[sl, :] = _dot_nt(wvt_ref[h], xm[:, sl]).astype(BF16)


def _convqkv(xm, width, cw, cb, wqt, wk, wvt):
    b, s, c = xm.shape
    rows = s // width
    tm = min(512, s)
    halo = LANES if rows > 1 else SUBLANES
    assert tm % width == 0 and tm % halo == 0 and (rows == 1 or halo > width)
    assert width & (width - 1) == 0
    per = tm // halo
    nblk = s // halo
    kern = functools.partial(_convqkv_kernel, width, rows, halo)
    tr = pl.BlockSpec((None, c, tm), lambda bi, i: (bi, 0, i))
    return pl.pallas_call(
        kern,
        grid=(b, s // tm),
        in_specs=[pl.BlockSpec((None, halo, c), lambda bi, i: (bi, jnp.maximum(i * per - 1, 0), 0)),
                  pl.BlockSpec((None, tm, c), lambda bi, i: (bi, i, 0)),
                  pl.BlockSpec((None, halo, c), lambda bi, i: (bi, jnp.minimum((i + 1) * per, nblk - 1), 0)),
                  _full(cw.shape), _full(cb.shape), _full(wqt.shape), _full(wk.shape), _full(wvt.shape)],
        out_specs=[tr, pl.BlockSpec((None, tm, c), lambda bi, i: (bi, i, 0)), tr],
        out_shape=[jax.ShapeDtypeStruct((b, c, s), BF16),
                   jax.ShapeDtypeStruct((b, s, c), BF16),
                   jax.ShapeDtypeStruct((b, c, s), BF16)],
        compiler_params=_cparams(2, 32),
        name="convqkv",
    )(xm, xm, xm, cw, cb, wqt, wk, wvt)


def _tri_ones():
    u = np.arange(CHUNK)[:, None]
    t = np.arange(CHUNK)[None, :]
    return jnp.asarray(np.concatenate([u <= t, u >= t], axis=0), BF16)


def _split_dot(x, w):
    hi = x.astype(BF16)
    r1 = x - hi.astype(F32)
    mid = r1.astype(BF16)
    lo = (r1 - mid.astype(F32)).astype(BF16)
    return _dot(hi, w) + _dot(mid, w) + _dot(lo, w)


def _mlstm_kernel(qtf_ref, kf_ref, vtf_ref, gtf_ref, qtb_ref, kb_ref, vtb_ref, gtb_ref,
                  bias_ref, tri_ref, c0_ref, m0_ref, htf_ref, htb_ref, cfin_ref, mfin_ref, c_st, m_st):
    j = pl.program_id(1)
    L = CHUNK
    hd = MLSTM_HD
    n_sub = gtf_ref.shape[1] // L

    @pl.when(j == 0)
    def _():
        c_st[...] = c0_ref[...]
        m_st[...] = m0_ref[...]

    row_fwd = lax.broadcasted_iota(jnp.int32, (N_HD, L), 0) < MLSTM_HEADS
    src_row = lax.broadcasted_iota(jnp.int32, (L, L), 0)
    out_col = lax.broadcasted_iota(jnp.int32, (L, L), 1)
    one_row = jnp.where(lax.broadcasted_iota(jnp.int32, (PACKED_SUBLANES, L), 0) == 0, 1.0, 0.0).astype(BF16)
    bcast = lambda col: jnp.broadcast_to(col, (N_HD, L))

    def operands(u, p):
        d, hh = divmod(p, MLSTM_HEADS)
        hs = slice(hh * hd, (hh + 1) * hd)
        if d == 0:
            tl = slice(u * L, (u + 1) * L)
            return qtf_ref[hs, tl], kf_ref[tl, hs], vtf_ref[hs, tl], htf_ref, hs, tl
        tl = slice((n_sub - 1 - u) * L, (n_sub - u) * L)
        return qtb_ref[hs, tl], kb_ref[tl, hs], vtb_ref[hs, tl], htb_ref, hs, tl

    gates = []
    for u in range(n_sub):
        fl = slice(u * L, (u + 1) * L)
        bl = slice((n_sub - 1 - u) * L, (n_sub - u) * L)
        ig = jnp.where(row_fwd, gtf_ref[:N_HD, fl], gtb_ref[:N_HD, bl]) + bias_ref[:N_HD]
        fp = jnp.where(row_fwd, gtf_ref[N_HD:, fl], gtb_ref[N_HD:, bl]) + bias_ref[N_HD:]
        lf = jax.nn.log_sigmoid(fp)
        lf2 = jnp.concatenate([jnp.where(row_fwd, lf, 0.0), jnp.where(row_fwd, 0.0, lf)], axis=1)
        bc = _split_dot(lf2, tri_ref[...])
        gs = ig - bc
        btot = jnp.where(row_fwd, bcast(bc[:, L - 1:L]), bcast(bc[:, 0:1]))
        gmax = bcast(jnp.max(gs, axis=1, keepdims=True))
        gates.append((bc, gs, btot, gmax, jnp.exp(gs - gmax)))

    scores, local = {}, {}
    for u in range(n_sub):
        w_st = gates[u][4]
        for p in range(N_HD):
            qt, k, vt, _, _, _ = operands(u, p)
            scores[u, p] = _dot(k, qt)
            vaug = jnp.concatenate([vt, one_row], axis=0)
            vw = (vaug.astype(F32) * w_st[p:p + 1, :]).astype(BF16)
            local[u, p] = _dot(vw, k)

    for u in range(n_sub):
        bc, gs, btot, gmax, _ = gates[u]
        m_prev = m_st[...]
        m_loc = btot + gmax
        m_new = jnp.maximum(btot + m_prev, m_loc)
        dec = jnp.exp(btot + m_prev - m_new)
        inj = jnp.exp(m_loc - m_new)
        m_st[...] = m_new
        for p in range(N_HD):
            qt, k, vt, dst, hs, tl = operands(u, p)
            seen = (src_row <= out_col) if p < MLSTM_HEADS else (src_row >= out_col)
            gs_seen = jnp.where(seen, jnp.broadcast_to(gs[p:p + 1, :], (L, L)).T, -jnp.inf)
            xm = jnp.maximum(m_prev[p:p + 1, :], jnp.max(gs_seen, axis=0, keepdims=True))
            pt = (scores[u, p] * jnp.exp(gs_seen - xm)).astype(BF16)
            qw = (qt.astype(F32) * jnp.exp(m_prev[p:p + 1, :] - xm)).astype(BF16)
            vaug = jnp.concatenate([vt, one_row], axis=0)
            c_prev = c_st[p]
            lhs = jnp.concatenate([vaug, c_prev.astype(BF16)], axis=1)
            rhs = jnp.concatenate([pt, qw], axis=0)
            o = _dot(lhs, rhs)
            den = jnp.maximum(jnp.abs(o[hd:hd + 1]), jnp.exp(-(bc[p:p + 1, :] + xm)))
            dst[hs, tl] = o[:hd] * (1.0 / den)
            c_st[p] = dec[p:p + 1, :] * c_prev + inj[p:p + 1, :] * local[u, p]

    @pl.when(j == pl.num_programs(1) - 1)
    def _():
        cfin_ref[...] = c_st[...]
        mfin_ref[...] = m_st[...]


def _mlstm(qt, k, vt, gt, bias, c0, m0):
    b, c, s = qt.shape
    tb = min(SCAN_CHUNKS * CHUNK, s)
    nb = s // tb
    tri = _tri_ones()
    fwd_t = lambda bi, j: (bi, 0, j)
    bwd_t = lambda bi, j: (bi, 0, nb - 1 - j)
    fwd_n = lambda bi, j: (bi, j, 0)
    bwd_n = lambda bi, j: (bi, nb - 1 - j, 0)
    st_c = pl.BlockSpec((None, N_HD, STATE_ROWS, MLSTM_HD), lambda bi, j: (bi, 0, 0, 0))
    st_m = pl.BlockSpec((None, N_HD, CHUNK), lambda bi, j: (bi, 0, 0))
    side = lambda ti, ni: [pl.BlockSpec((None, c, tb), ti), pl.BlockSpec((None, tb, c), ni),
                           pl.BlockSpec((None, c, tb), ti), pl.BlockSpec((None, GATE_ROWS, tb), ti)]
    return pl.pallas_call(
        _mlstm_kernel,
        grid=(b, nb),
        in_specs=side(fwd_t, fwd_n) + side(bwd_t, bwd_n) + [_full(bias.shape), _full(tri.shape), st_c, st_m],
        out_specs=[pl.BlockSpec((None, c, tb), fwd_t), pl.BlockSpec((None, c, tb), bwd_t), st_c, st_m],
        out_shape=[jax.ShapeDtypeStruct((b, c, s), F32),
                   jax.ShapeDtypeStruct((b, c, s), F32),
                   jax.ShapeDtypeStruct(c0.shape, F32),
                   jax.ShapeDtypeStruct(m0.shape, F32)],
        scratch_shapes=[pltpu.VMEM((N_HD, STATE_ROWS, MLSTM_HD), F32),
                        pltpu.VMEM((N_HD, CHUNK), F32)],
        compiler_params=_cparams(2, 56),
        name="mlstm",
    )(qt, k, vt, gt, qt, k, vt, gt, bias, tri, c0, m0)


DFT_GROUP = SUBLANES


def _dft_tables(s, n2):
    n1 = s // n2
    k1 = np.arange(n1, dtype=np.float64)
    s1 = np.arange(n1, dtype=np.float64)
    ma = np.zeros((n2, 2 * n1, 2 * n1), np.float64)
    for s2 in range(n2):
        th = 2.0 * np.pi * np.outer(k1, s1 * n2 + s2) / s
        c, sn = np.cos(th), np.sin(th)
        ma[s2] = np.block([[c, sn], [-sn, c]])
    k2 = np.arange(n2, dtype=np.float64)
    ph = 2.0 * np.pi * np.outer(k2, k2) / n2
    eye = np.eye(DFT_GROUP)
    scale = 1.0 / math.sqrt(s)
    lb = np.einsum("ksr,ab->kasrb", np.stack([np.cos(ph), np.sin(ph)], axis=-1) * scale, eye)
    lb = lb.reshape(n2 * DFT_GROUP, n2 * 2 * DFT_GROUP)
    return jnp.asarray(ma, F32).astype(BF16), jnp.asarray(lb, F32).astype(BF16)


def _seq_dft_kernel(n_a, z_ref, ma_ref, lb_ref, y_ref, t_scr):
    j = pl.program_id(1)
    g = DFT_GROUP
    n2 = t_scr.shape[0]
    n1 = t_scr.shape[2]

    @pl.when(j < n_a)
    def _():
        for s in range(g):
            zs = z_ref[:, s, :]
            rhs = jnp.concatenate([zs[:, :D_FOURIER], zs[:, D_FOURIER:]], axis=0).astype(BF16)
            t_scr[j * g + s] = _dot(ma_ref[s], rhs).reshape(2, n1, D_FOURIER)

    @pl.when(j >= n_a)
    def _():
        k0 = pl.multiple_of((j - n_a) * g, g)
        rhs = t_scr[:, :, pl.ds(k0, g), :].reshape(n2 * 2 * g, D_FOURIER).astype(BF16)
        y_ref[...] = _dot(lb_ref[...], rhs).reshape(n2, g, D_FOURIER)


def _seq_dft(z, n2=64):
    b, s, w = z.shape
    n1 = s // n2
    g = DFT_GROUP
    n_a = n2 // g
    n_b = n1 // g
    ma, lb = _dft_tables(s, n2)
    y = pl.pallas_call(
        functools.partial(_seq_dft_kernel, n_a),
        grid=(b, n_a + n_b),
        in_specs=[pl.BlockSpec((None, n1, g, w), lambda bi, j: (bi, 0, jnp.minimum(j, n_a - 1), 0)),
                  pl.BlockSpec((g, 2 * n1, 2 * n1), lambda bi, j: (jnp.minimum(j, n_a - 1), 0, 0)),
                  _full(lb.shape)],
        out_specs=pl.BlockSpec((None, n2, g, D_FOURIER), lambda bi, j: (bi, 0, jnp.maximum(j - n_a, 0), 0)),
        out_shape=jax.ShapeDtypeStruct((b, n2, n1, D_FOURIER), F32),
        scratch_shapes=[pltpu.VMEM((n2, 2, n1, D_FOURIER), F32)],
        compiler_params=_cparams(2, 60),
        name="seq_dft",
    )(z.reshape(b, n1, n2, w), ma, lb)
    return y.reshape(b, s, D_FOURIER)


def _dense_dft_kernel(z_ref, ld_ref, y_ref):
    rhs = jnp.concatenate([z_ref[:, :D_FOURIER], z_ref[:, D_FOURIER:]], axis=0).astype(BF16)
    y_ref[...] = _dot(ld_ref[...], rhs)


def _dense_dft(z):
    b, s, w = z.shape
    k = np.arange(s, dtype=np.float64)
    th = 2.0 * np.pi * np.outer(k, k) / s
    ld = jnp.asarray(np.concatenate([np.cos(th), np.sin(th)], axis=1) / math.sqrt(s), F32).astype(BF16)
    return pl.pallas_call(
        _dense_dft_kernel,
        grid=(b,),
        in_specs=[pl.BlockSpec((None, s, w), lambda bi: (bi, 0, 0)), _full(ld.shape)],
        out_specs=pl.BlockSpec((None, s, D_FOURIER), lambda bi: (bi, 0, 0)),
        out_shape=jax.ShapeDtypeStruct((b, s, D_FOURIER), F32),
        compiler_params=_cparams(1, 32),
        name="dft_dense",
    )(z, ld)


def _combine_ffn_kernel(x_ref, yf_ref, htf_ref, htb_ref, zt_ref, mod_ref, nw_ref, hnw_ref, wom_ref,
                        wg_ref, wu_ref, wo_ref, o_ref, a_ref):
    mod = mod_ref[...]
    _, _, gate = _mod3(mod, 1)
    hs = htf_ref[...] + htb_ref[...]
    hd = MLSTM_HD
    parts = []
    for h in range(MLSTM_HEADS):
        seg = hs[h * hd:(h + 1) * hd]
        mu = jnp.mean(seg, axis=0, keepdims=True)
        cen = seg - mu
        var = jnp.mean(cen * cen, axis=0, keepdims=True)
        parts.append(cen * lax.rsqrt(var + EPS))
    hn = jnp.concatenate(parts, axis=0) * hnw_ref[...]
    ym = (hn * jax.nn.sigmoid(zt_ref[...])).T.astype(BF16)
    y = _dot(yf_ref[...].astype(BF16), wom_ref[:D_FOURIER]) + _dot(ym, wom_ref[D_FOURIER:])
    x2 = x_ref[...] + gate * _rms(y, nw_ref[3:4])
    o_ref[...] = _ffn_body(x2, mod, 2, nw_ref, wg_ref, wu_ref, wo_ref, a_ref)


def _combine_ffn(x, yf, htf, htb, zt, mod, nw, hnw, wom, wg, wu, wo):
    b, s, d = x.shape
    tm = min(512, s)
    tok = lambda w: pl.BlockSpec((None, tm, w), lambda bi, i: (bi, i, 0))
    tr = pl.BlockSpec((None, D_MLSTM, tm), lambda bi, i: (bi, 0, i))
    hnw_b = jnp.broadcast_to(hnw.reshape(D_MLSTM, 1), (D_MLSTM, tm))
    return pl.pallas_call(
        _combine_ffn_kernel,
        grid=(b, s // tm),
        in_specs=[tok(d), tok(D_FOURIER), tr, tr, tr,
                  pl.BlockSpec((None, N_MOD, d), _mod_index(mod)),
                  _resident(nw.shape), _resident(hnw_b.shape), _resident(wom.shape),
                  _resident(wg.shape), _resident(wu.shape), _resident(wo.shape)],
        out_specs=tok(d),
        out_shape=jax.ShapeDtypeStruct(x.shape, F32),
        scratch_shapes=[pltpu.VMEM((tm, D_FF), BF16)],
        compiler_params=_cparams(2, 60),
        name="combine_ffn",
    )(x, yf, htf, htb, zt, mod, nw, hnw_b, wom, wg, wu, wo)


def _gate_layout(w_gate, gate_b):
    h = MLSTM_HEADS
    wt = w_gate.T
    w = jnp.concatenate([wt[0:h], wt[2 * h:3 * h], wt[h:2 * h], wt[3 * h:4 * h]], axis=0)
    bias = jnp.concatenate([gate_b[0], gate_b[2], gate_b[1], gate_b[3]])
    return w, jnp.broadcast_to(bias.reshape(GATE_ROWS, 1), (GATE_ROWS, CHUNK))


def kernel(x, c, ctx, c_ctx, w_ada, b_ada, norm_w, w_ff_in, w_ff_out, w_in, w_fmix,
           conv_w, conv_b, w_qkv, gate_b, mlstm_norm_w, w_out):
    b = x.shape[0]
    d = D_MODEL
    n_cond = b + 1
    assert n_cond <= SUBLANES
    cc = jnp.concatenate([c, c_ctx[None], jnp.zeros((SUBLANES - n_cond, d), F32)], axis=0)
    mod = _modulation(cc, w_ada, b_ada).reshape(DEPTH, SUBLANES, N_MOD, d)
    w_fold = _fold_fourier(w_in, w_fmix)

    xl, xc = x, ctx
    c_zero = jnp.zeros((b, N_HD, STATE_ROWS, MLSTM_HD), F32)
    m_zero = jnp.zeros((b, N_HD, CHUNK), F32)
    for l in range(DEPTH):
        last = l == DEPTH - 1
        mod_l = mod[l, :b]
        mod_c = mod[l, b:b + 1]
        nw = norm_w[l]
        ffw = [(w_ff_in[l, i, :, :D_FF].astype(BF16), w_ff_in[l, i, :, D_FF:].astype(BF16),
                w_ff_out[l, i].astype(BF16)) for i in range(2)]
        o_xm = D_FOURIER
        o_z = D_FOURIER + D_MLSTM
        o_g = D_FOURIER + 2 * D_MLSTM
        w_gate, g_bias = _gate_layout(w_in[l, :, o_g:], gate_b[l])
        pw = (w_fold[l], w_in[l, :, o_xm:o_z].astype(BF16),
              jnp.concatenate([w_in[l, :, o_z:o_g].T, w_gate], axis=0).astype(BF16))
        cw = conv_w[l].reshape(9, D_MLSTM)
        cb = conv_b[l].reshape(1, D_MLSTM)
        qkvw = (jnp.swapaxes(w_qkv[l, 0], 1, 2).astype(BF16), w_qkv[l, 1].astype(BF16),
                jnp.swapaxes(w_qkv[l, 2], 1, 2).astype(BF16))
        hnw = mlstm_norm_w[l]
        wo = w_out[l].astype(BF16)

        xl, z_l, xm_l, zt_l, gt_l = _ffn_inproj(xl, mod_l, nw, *ffw[0], *pw)
        xc, z_c, xm_c, zt_c, gt_c = _ffn_inproj(xc, mod_c, nw, *ffw[0], *pw)
        qkv_l = _convqkv(xm_l, GRID_W, cw, cb, *qkvw)
        qkv_c = _convqkv(xm_c, xm_c.shape[1], cw, cb, *qkvw)
        htf_c, htb_c, c_ctx_st, m_ctx_st = _mlstm(*qkv_c, gt_c, g_bias, c_zero, m_zero)
        htf_l, htb_l, _, _ = _mlstm(*qkv_l, gt_l, g_bias, c_ctx_st, m_ctx_st)
        yf_l = _seq_dft(z_l)
        xl = _combine_ffn(xl, yf_l, htf_l, htb_l, zt_l, mod_l, nw, hnw, wo, *ffw[1])
        if not last:
            yf_c = _dense_dft(z_c)
            xc = _combine_ffn(xc, yf_c, htf_c, htb_c, zt_c, mod_c, nw, hnw, wo, *ffw[1])
    return xl
```

```python
import functools
import math

import numpy as np
import jax
import jax.numpy as jnp
from jax import lax
from jax.experimental import pallas as pl
from jax.experimental.pallas import tpu as pltpu

D_MODEL = 1024
DEPTH = 2
GRID_W = 64
D_FOURIER = 512
FOURIER_GROUPS = 8
FOURIER_GW = 64
D_MLSTM = 512
MLSTM_HEADS = 4
MLSTM_HD = 128
CHUNK = 128
D_FF = 2816
N_MOD = 9
FFN_RES = 0.5
EPS = 1e-6

LANES = 128
SUBLANES = 8
PACKED_SUBLANES = 16
MXU_DIM = 256
VMEM_BYTES = 64 * 1024 * 1024
N_HD = 2 * MLSTM_HEADS
STATE_ROWS = MLSTM_HD + PACKED_SUBLANES
GATE_ROWS = 2 * N_HD
SCAN_CHUNKS = 8

F32 = jnp.float32
BF16 = jnp.bfloat16
NT = (((1,), (1,)), ((), ()))


def _cparams(n_grid, vmem_mb):
    return pltpu.CompilerParams(
        dimension_semantics=("arbitrary",) * n_grid,
        vmem_limit_bytes=min(vmem_mb * 1024 * 1024, VMEM_BYTES - 4 * 1024 * 1024))


def _dot(a, b):
    return jnp.dot(a, b, preferred_element_type=F32)


def _dot_nt(a, b):
    return lax.dot_general(a, b, NT, preferred_element_type=F32)


def _rms(x, w):
    return x * lax.rsqrt(jnp.mean(x * x, axis=-1, keepdims=True) + EPS) * w


def _mod3(m, j):
    return m[3 * j:3 * j + 1], m[3 * j + 1:3 * j + 2], m[3 * j + 2:3 * j + 3]


def _mod_index(mod):
    if mod.shape[0] == 1:
        return lambda b, i: (0, 0, 0)
    return lambda b, i: (b, 0, 0)


def _full(shape):
    return pl.BlockSpec(shape, lambda *_: (0,) * len(shape))


def _resident(shape, lead=()):
    rest = tuple(shape[len(lead):])
    index = tuple(lead) + (0,) * len(rest)
    return pl.BlockSpec((None,) * len(lead) + rest, lambda *_: index, pipeline_mode=pl.Buffered(1))


def _mod_kernel(c_ref, w_ref, b_ref, o_ref):
    c = c_ref[...]
    s = (c * jax.nn.sigmoid(c)).astype(BF16)
    o_ref[...] = _dot(s, w_ref[...].astype(BF16)) + b_ref[...]


def _modulation(cc, w_ada, b_ada):
    d = D_MODEL
    return pl.pallas_call(
        _mod_kernel,
        grid=(DEPTH, N_MOD),
        in_specs=[pl.BlockSpec((SUBLANES, d), lambda l, n: (0, 0)),
                  pl.BlockSpec((None, d, d), lambda l, n: (l, 0, n)),
                  pl.BlockSpec((None, 1, d), lambda l, n: (l, 0, n))],
        out_specs=pl.BlockSpec((None, SUBLANES, d), lambda l, n: (l, 0, n)),
        out_shape=jax.ShapeDtypeStruct((DEPTH, SUBLANES, N_MOD * d), F32),
        compiler_params=_cparams(2, 32),
        name="modulation",
    )(cc, w_ada, b_ada.reshape(DEPTH, 1, N_MOD * d))


FF_CHUNK = MXU_DIM
N_FF_CHUNKS = D_FF // FF_CHUNK


def _ffn_body(x, mod, j, nw_ref, wi_ref, wo_ref, a_ref):
    shift, scale, gate = _mod3(mod, j)
    h = (_rms(x, nw_ref[2 * j:2 * j + 1]) * (1.0 + scale) + shift).astype(BF16)
    for c in range(N_FF_CHUNKS):
        sl = slice(c * FF_CHUNK, (c + 1) * FF_CHUNK)
        g = _dot(h, wi_ref[:, sl])
        u = _dot(h, wi_ref[:, D_FF + c * FF_CHUNK:D_FF + (c + 1) * FF_CHUNK])
        a_ref[:, sl] = (g * jax.nn.sigmoid(g) * u).astype(BF16)
    y = _dot(a_ref[...], wo_ref[...])
    return x + FFN_RES * gate * _rms(y, nw_ref[2 * j + 1:2 * j + 2])


def _fold_kernel(wf_ref, bdw_ref, bdc_ref, bds_ref, o_ref):
    hi = lax.Precision.HIGHEST
    wf = wf_ref[...]
    bdw = bdw_ref[...]
    gr = jnp.dot(bdc_ref[...], bdw, precision=hi, preferred_element_type=F32)
    gi = jnp.dot(bds_ref[...], bdw, precision=hi, preferred_element_type=F32)
    o_ref[:, :D_FOURIER] = jnp.dot(wf, gr, precision=hi, preferred_element_type=F32).astype(BF16)
    o_ref[:, D_FOURIER:] = jnp.dot(wf, gi, precision=hi, preferred_element_type=F32).astype(BF16)


def _channel_dft_blocks():
    c = np.arange(FOURIER_GW, dtype=np.float64)
    ang = 2.0 * np.pi * np.outer(c, c) / FOURIER_GW
    eye = np.eye(FOURIER_GROUPS)
    scale = 1.0 / math.sqrt(FOURIER_GW)
    bdc = np.kron(eye, np.cos(ang) * scale)
    bds = np.kron(eye, -np.sin(ang) * scale)
    return jnp.asarray(bdc, F32), jnp.asarray(bds, F32)


def _fold_fourier(w_in, w_fmix):
    d = D_MODEL
    eye = jnp.eye(FOURIER_GROUPS, dtype=F32)
    bdw = jnp.einsum("gh,lgcd->lgchd", eye, w_fmix).reshape(DEPTH, D_FOURIER, D_FOURIER)
    bdc, bds = _channel_dft_blocks()
    sq = pl.BlockSpec((D_FOURIER, D_FOURIER), lambda l: (0, 0))
    return pl.pallas_call(
        _fold_kernel,
        grid=(DEPTH,),
        in_specs=[pl.BlockSpec((None, d, D_FOURIER), lambda l: (l, 0, 0)),
                  pl.BlockSpec((None, D_FOURIER, D_FOURIER), lambda l: (l, 0, 0)),
                  sq, sq],
        out_specs=pl.BlockSpec((None, d, 2 * D_FOURIER), lambda l: (l, 0, 0)),
        out_shape=jax.ShapeDtypeStruct((DEPTH, d, 2 * D_FOURIER), BF16),
        compiler_params=_cparams(1, 32),
        name="fold_fourier",
    )(w_in, bdw, bdc, bds)


def _ffn_inproj_kernel(x_ref, mod_ref, nw_ref, wi_ref, wo_ref, wz_ref, wxm_ref, wt_ref,
                       o_ref, z_ref, xm_ref, zt_ref, gt_ref, a_ref):
    mod = mod_ref[...]
    x1 = _ffn_body(x_ref[...], mod, 0, nw_ref, wi_ref, wo_ref, a_ref)
    o_ref[...] = x1
    shift, scale, _ = _mod3(mod, 1)
    h = (_rms(x1, nw_ref[2:3]) * (1.0 + scale) + shift).astype(BF16)
    z_ref[...] = _dot(h, wz_ref[...])
    xm_ref[...] = _dot(h, wxm_ref[...])
    t = _dot_nt(wt_ref[...], h)
    zt_ref[...] = t[:D_MLSTM]
    gt_ref[...] = t[D_MLSTM:]


def _ffn_inproj(x, mod, l, norm_w, ff_in, ff_out, wz, wxm, wt):
    b, s, d = x.shape
    tm = min(512, s)
    tok = lambda w: pl.BlockSpec((None, tm, w), lambda bi, i: (bi, i, 0))
    tr = lambda r: pl.BlockSpec((None, r, tm), lambda bi, i: (bi, 0, i))
    return pl.pallas_call(
        _ffn_inproj_kernel,
        grid=(b, s // tm),
        in_specs=[tok(d), pl.BlockSpec((None, N_MOD, d), _mod_index(mod)), _resident(norm_w.shape, (l,)),
                  _resident(ff_in.shape, (l, 0)), _resident(ff_out.shape, (l, 0)),
                  _resident(wz.shape, (l,)), _resident(wxm.shape), _resident(wt.shape)],
        out_specs=[tok(d), tok(2 * D_FOURIER), tok(D_MLSTM), tr(D_MLSTM), tr(GATE_ROWS)],
        out_shape=[jax.ShapeDtypeStruct(x.shape, F32),
                   jax.ShapeDtypeStruct((b, s, 2 * D_FOURIER), F32),
                   jax.ShapeDtypeStruct((b, s, D_MLSTM), F32),
                   jax.ShapeDtypeStruct((b, D_MLSTM, s), F32),
                   jax.ShapeDtypeStruct((b, GATE_ROWS, s), F32)],
        scratch_shapes=[pltpu.VMEM((tm, D_FF), BF16)],
        compiler_params=_cparams(2, 60),
        name="ffn_inproj",
    )(x, mod, norm_w, ff_in, ff_out, wz, wxm, wt)


def _convqkv_kernel(width, rows, halo, prev_ref, cur_ref, next_ref, cw_ref, cb_ref,
                    wqt_ref, wk_ref, wvt_ref, qt_ref, k_ref, vt_ref):
    i = pl.program_id(1)
    tm = cur_ref.shape[0]
    cur = cur_ref[...]
    prev = jnp.where(i > 0, prev_ref[...], 0.0)
    nxt = jnp.where(i < pl.num_programs(1) - 1, next_ref[...], 0.0)
    ext = jnp.concatenate([prev, cur, nxt], axis=0)
    n = ext.shape[0]
    r = lax.broadcasted_iota(jnp.int32, ext.shape, 0)
    col = (r + (width - halo % width)) & (width - 1)
    left = jnp.where(col == 0, 0.0, pltpu.roll(ext, 1, axis=0))
    right = jnp.where(col == width - 1, 0.0, pltpu.roll(ext, n - 1, axis=0))
    taps = (left, ext, right)
    cw = cw_ref[...]
    acc = jnp.zeros((tm, cur.shape[1]), F32) + cb_ref[...]
    for dy in (-1, 0, 1):
        if rows == 1 and dy != 0:
            continue
        lo = halo + dy * width
        for dx in (-1, 0, 1):
            k = (dy + 1) * 3 + (dx + 1)
            acc = acc + cw[k:k + 1] * taps[dx + 1][lo:lo + tm]
    cv = (acc * jax.nn.sigmoid(acc)).astype(BF16)
    xm = cur.astype(BF16)
    hd = MLSTM_HD
    for h in range(MLSTM_HEADS):
        sl = slice(h * hd, (h + 1) * hd)
        qt_ref[sl, :] = _dot_nt(wqt_ref[h], cv[:, sl]).astype(BF16)
        k_ref[:, sl] = (_dot(cv[:, sl], wk_ref[h]) * hd ** -0.5).astype(BF16)
        vt_ref[sl, :] = _dot_nt(wvt_ref[h], xm[:, sl]).astype(BF16)


def _convqkv(xm, width, cw, cb, wqt, wk, wvt):
    b, s, c = xm.shape
    rows = s // width
    tm = min(512, s)
    halo = LANES if rows > 1 else SUBLANES
    assert tm % width == 0 and tm % halo == 0 and (rows == 1 or halo > width)
    assert width & (width - 1) == 0
    per = tm // halo
    nblk = s // halo
    kern = functools.partial(_convqkv_kernel, width, rows, halo)
    tr = pl.BlockSpec((None, c, tm), lambda bi, i: (bi, 0, i))
    return pl.pallas_call(
        kern,
        grid=(b, s // tm),
        in_specs=[pl.BlockSpec((None, halo, c), lambda bi, i: (bi, jnp.maximum(i * per - 1, 0), 0)),
                  pl.BlockSpec((None, tm, c), lambda bi, i: (bi, i, 0)),
                  pl.BlockSpec((None, halo, c), lambda bi, i: (bi, jnp.minimum((i + 1) * per, nblk - 1), 0)),
                  _full(cw.shape), _full(cb.shape), _full(wqt.shape), _full(wk.shape), _full(wvt.shape)],
        out_specs=[tr, pl.BlockSpec((None, tm, c), lambda bi, i: (bi, i, 0)), tr],
        out_shape=[jax.ShapeDtypeStruct((b, c, s), BF16),
                   jax.ShapeDtypeStruct((b, s, c), BF16),
                   jax.ShapeDtypeStruct((b, c, s), BF16)],
        compiler_params=_cparams(2, 32),
        name="convqkv",
    )(xm, xm, xm, cw, cb, wqt, wk, wvt)


def _tri_ones():
    u = np.arange(CHUNK)[:, None]
    t = np.arange(CHUNK)[None, :]
    return jnp.asarray(np.concatenate([u <= t, u >= t], axis=0), BF16)


def _split_dot(x, w):
    hi = x.astype(BF16)
    r1 = x - hi.astype(F32)
    mid = r1.astype(BF16)
    lo = (r1 - mid.astype(F32)).astype(BF16)
    return _dot(hi, w) + _dot(mid, w) + _dot(lo, w)


def _mlstm_kernel(qtf_ref, kf_ref, vtf_ref, gtf_ref, qtb_ref, kb_ref, vtb_ref, gtb_ref,
                  bias_ref, tri_ref, c0_ref, m0_ref, htf_ref, htb_ref, cfin_ref, mfin_ref, c_st, m_st):
    j = pl.program_id(1)
    L = CHUNK
    hd = MLSTM_HD
    n_sub = gtf_ref.shape[1] // L

    @pl.when(j == 0)
    def _():
        c_st[...] = c0_ref[...]
        m_st[...] = m0_ref[...]

    row_fwd = lax.broadcasted_iota(jnp.int32, (N_HD, L), 0) < MLSTM_HEADS
    src_row = lax.broadcasted_iota(jnp.int32, (L, L), 0)
    out_col = lax.broadcasted_iota(jnp.int32, (L, L), 1)
    one_row = jnp.where(lax.broadcasted_iota(jnp.int32, (PACKED_SUBLANES, L), 0) == 0, 1.0, 0.0).astype(BF16)
    bcast = lambda col: jnp.broadcast_to(col, (N_HD, L))

    def operands(u, p):
        d, hh = divmod(p, MLSTM_HEADS)
        hs = slice(hh * hd, (hh + 1) * hd)
        if d == 0:
            tl = slice(u * L, (u + 1) * L)
            return qtf_ref[hs, tl], kf_ref[tl, hs], vtf_ref[hs, tl], htf_ref, hs, tl
        tl = slice((n_sub - 1 - u) * L, (n_sub - u) * L)
        return qtb_ref[hs, tl], kb_ref[tl, hs], vtb_ref[hs, tl], htb_ref, hs, tl

    gates = []
    for u in range(n_sub):
        fl = slice(u * L, (u + 1) * L)
        bl = slice((n_sub - 1 - u) * L, (n_sub - u) * L)
        ig = jnp.where(row_fwd, gtf_ref[:N_HD, fl], gtb_ref[:N_HD, bl]) + bias_ref[:N_HD]
        fp = jnp.where(row_fwd, gtf_ref[N_HD:, fl], gtb_ref[N_HD:, bl]) + bias_ref[N_HD:]
        lf = jax.nn.log_sigmoid(fp)
        lf2 = jnp.concatenate([jnp.where(row_fwd, lf, 0.0), jnp.where(row_fwd, 0.0, lf)], axis=1)
        bc = _split_dot(lf2, tri_ref[...])
        gs = ig - bc
        btot = jnp.where(row_fwd, bcast(bc[:, L - 1:L]), bcast(bc[:, 0:1]))
        gmax = bcast(jnp.max(gs, axis=1, keepdims=True))
        gates.append((bc, gs, btot, gmax, jnp.exp(gs - gmax)))

    scores, local = {}, {}
    for u in range(n_sub):
        w_st = gates[u][4]
        for p in range(N_HD):
            qt, k, vt, _, _, _ = operands(u, p)
            scores[u, p] = _dot(k, qt)
            vaug = jnp.concatenate([vt, one_row], axis=0)
            vw = (vaug.astype(F32) * w_st[p:p + 1, :]).astype(BF16)
            local[u, p] = _dot(vw, k)

    for u in range(n_sub):
        bc, gs, btot, gmax, _ = gates[u]
        m_prev = m_st[...]
        m_loc = btot + gmax
        m_new = jnp.maximum(btot + m_prev, m_loc)
        dec = jnp.exp(btot + m_prev - m_new)
        inj = jnp.exp(m_loc - m_new)
        m_st[...] = m_new
        for p in range(N_HD):
            qt, k, vt, dst, hs, tl = operands(u, p)
            seen = (src_row <= out_col) if p < MLSTM_HEADS else (src_row >= out_col)
            gs_seen = jnp.where(seen, jnp.broadcast_to(gs[p:p + 1, :], (L, L)).T, -jnp.inf)
            xm = jnp.maximum(m_prev[p:p + 1, :], jnp.max(gs_seen, axis=0, keepdims=True))
            pt = (scores[u, p] * jnp.exp(gs_seen - xm)).astype(BF16)
            qw = (qt.astype(F32) * jnp.exp(m_prev[p:p + 1, :] - xm)).astype(BF16)
            vaug = jnp.concatenate([vt, one_row], axis=0)
            c_prev = c_st[p]
            lhs = jnp.concatenate([vaug, c_prev.astype(BF16)], axis=1)
            rhs = jnp.concatenate([pt, qw], axis=0)
            o = _dot(lhs, rhs)
            den = jnp.maximum(jnp.abs(o[hd:hd + 1]), jnp.exp(-(bc[p:p + 1, :] + xm)))
            dst[hs, tl] = o[:hd] * (1.0 / den)
            c_st[p] = dec[p:p + 1, :] * c_prev + inj[p:p + 1, :] * local[u, p]

    @pl.when(j == pl.num_programs(1) - 1)
    def _():
        cfin_ref[...] = c_st[...]
        mfin_ref[...] = m_st[...]


def _mlstm(qt, k, vt, gt, bias, c0, m0):
    b, c, s = qt.shape
    tb = min(SCAN_CHUNKS * CHUNK, s)
    nb = s // tb
    tri = _tri_ones()
    fwd_t = lambda bi, j: (bi, 0, j)
    bwd_t = lambda bi, j: (bi, 0, nb - 1 - j)
    fwd_n = lambda bi, j: (bi, j, 0)
    bwd_n = lambda bi, j: (bi, nb - 1 - j, 0)
    st_c = pl.BlockSpec((None, N_HD, STATE_ROWS, MLSTM_HD), lambda bi, j: (bi, 0, 0, 0))
    st_m = pl.BlockSpec((None, N_HD, CHUNK), lambda bi, j: (bi, 0, 0))
    side = lambda ti, ni: [pl.BlockSpec((None, c, tb), ti), pl.BlockSpec((None, tb, c), ni),
                           pl.BlockSpec((None, c, tb), ti), pl.BlockSpec((None, GATE_ROWS, tb), ti)]
    return pl.pallas_call(
        _mlstm_kernel,
        grid=(b, nb),
        in_specs=side(fwd_t, fwd_n) + side(bwd_t, bwd_n) + [_full(bias.shape), _full(tri.shape), st_c, st_m],
        out_specs=[pl.BlockSpec((None, c, tb), fwd_t), pl.BlockSpec((None, c, tb), bwd_t), st_c, st_m],
        out_shape=[jax.ShapeDtypeStruct((b, c, s), F32),
                   jax.ShapeDtypeStruct((b, c, s), F32),
                   jax.ShapeDtypeStruct(c0.shape, F32),
                   jax.ShapeDtypeStruct(m0.shape, F32)],
        scratch_shapes=[pltpu.VMEM((N_HD, STATE_ROWS, MLSTM_HD), F32),
                        pltpu.VMEM((N_HD, CHUNK), F32)],
        compiler_params=_cparams(2, 56),
        name="mlstm",
    )(qt, k, vt, gt, qt, k, vt, gt, bias, tri, c0, m0)


DFT_GROUP = SUBLANES


def _dft_tables(s, n2):
    n1 = s // n2
    k1 = np.arange(n1, dtype=np.float64)
    s1 = np.arange(n1, dtype=np.float64)
    ma = np.zeros((n2, 2 * n1, 2 * n1), np.float64)
    for s2 in range(n2):
        th = 2.0 * np.pi * np.outer(k1, s1 * n2 + s2) / s
        c, sn = np.cos(th), np.sin(th)
        ma[s2] = np.block([[c, sn], [-sn, c]])
    k2 = np.arange(n2, dtype=np.float64)
    ph = 2.0 * np.pi * np.outer(k2, k2) / n2
    eye = np.eye(DFT_GROUP)
    scale = 1.0 / math.sqrt(s)
    lb = np.einsum("ksr,ab->kasrb", np.stack([np.cos(ph), np.sin(ph)], axis=-1) * scale, eye)
    lb = lb.reshape(n2 * DFT_GROUP, n2 * 2 * DFT_GROUP)
    return jnp.asarray(ma, F32).astype(BF16), jnp.asarray(lb, F32).astype(BF16)


def _seq_dft_kernel(n_a, z_ref, ma_ref, lb_ref, y_ref, t_scr):
    j = pl.program_id(1)
    g = DFT_GROUP
    n2 = t_scr.shape[0]
    n1 = t_scr.shape[2]

    @pl.when(j < n_a)
    def _():
        for s in range(g):
            zs = z_ref[:, s, :]
            rhs = jnp.concatenate([zs[:, :D_FOURIER], zs[:, D_FOURIER:]], axis=0).astype(BF16)
            t_scr[j * g + s] = _dot(ma_ref[s], rhs).reshape(2, n1, D_FOURIER)

    @pl.when(j >= n_a)
    def _():
        for q in range(y_ref.shape[1] // g):
            k0 = pl.multiple_of((j - n_a) * y_ref.shape[1] + q * g, g)
            rhs = t_scr[:, :, pl.ds(k0, g), :].reshape(n2 * 2 * g, D_FOURIER).astype(BF16)
            y_ref[:, q * g:(q + 1) * g, :] = _dot(lb_ref[...], rhs).reshape(n2, g, D_FOURIER)


def _seq_dft(z, n2=64):
    b, s, w = z.shape
    n1 = s // n2
    g = DFT_GROUP
    n_a = n2 // g
    gb = 2 * g
    n_b = n1 // gb
    ma, lb = _dft_tables(s, n2)
    y = pl.pallas_call(
        functools.partial(_seq_dft_kernel, n_a),
        grid=(b, n_a + n_b),
        in_specs=[pl.BlockSpec((None, n1, g, w), lambda bi, j: (bi, 0, jnp.minimum(j, n_a - 1), 0)),
                  pl.BlockSpec((g, 2 * n1, 2 * n1), lambda bi, j: (jnp.minimum(j, n_a - 1), 0, 0)),
                  _full(lb.shape)],
        out_specs=pl.BlockSpec((None, n2, gb, D_FOURIER), lambda bi, j: (bi, 0, jnp.maximum(j - n_a, 0), 0)),
        out_shape=jax.ShapeDtypeStruct((b, n2, n1, D_FOURIER), F32),
        scratch_shapes=[pltpu.VMEM((n2, 2, n1, D_FOURIER), F32)],
        compiler_params=_cparams(2, 60),
        name="seq_dft",
    )(z.reshape(b, n1, n2, w), ma, lb)
    return y.reshape(b, s, D_FOURIER)


def _dense_dft_kernel(z_ref, ld_ref, y_ref):
    rhs = jnp.concatenate([z_ref[:, :D_FOURIER], z_ref[:, D_FOURIER:]], axis=0).astype(BF16)
    y_ref[...] = _dot(ld_ref[...], rhs)


def _dense_dft(z):
    b, s, w = z.shape
    k = np.arange(s, dtype=np.float64)
    th = 2.0 * np.pi * np.outer(k, k) / s
    ld = jnp.asarray(np.concatenate([np.cos(th), np.sin(th)], axis=1) / math.sqrt(s), F32).astype(BF16)
    return pl.pallas_call(
        _dense_dft_kernel,
        grid=(b,),
        in_specs=[pl.BlockSpec((None, s, w), lambda bi: (bi, 0, 0)), _full(ld.shape)],
        out_specs=pl.BlockSpec((None, s, D_FOURIER), lambda bi: (bi, 0, 0)),
        out_shape=jax.ShapeDtypeStruct((b, s, D_FOURIER), F32),
        compiler_params=_cparams(1, 32),
        name="dft_dense",
    )(z, ld)


def _combine_ffn_kernel(x_ref, yf_ref, htf_ref, htb_ref, zt_ref, mod_ref, nw_ref, hnw_ref, wom_ref,
                        wi_ref, wo_ref, o_ref, a_ref):
    mod = mod_ref[...]
    _, _, gate = _mod3(mod, 1)
    hs = htf_ref[...] + htb_ref[...]
    hd = MLSTM_HD
    parts = []
    for h in range(MLSTM_HEADS):
        seg = hs[h * hd:(h + 1) * hd]
        mu = jnp.mean(seg, axis=0, keepdims=True)
        cen = seg - mu
        var = jnp.mean(cen * cen, axis=0, keepdims=True)
        parts.append(cen * lax.rsqrt(var + EPS))
    hn = jnp.concatenate(parts, axis=0) * hnw_ref[...]
    ym = (hn * jax.nn.sigmoid(zt_ref[...])).T.astype(BF16)
    y = _dot(yf_ref[...].astype(BF16), wom_ref[:D_FOURIER]) + _dot(ym, wom_ref[D_FOURIER:])
    x2 = x_ref[...] + gate * _rms(y, nw_ref[3:4])
    o_ref[...] = _ffn_body(x2, mod, 2, nw_ref, wi_ref, wo_ref, a_ref)


def _combine_ffn(x, yf, htf, htb, zt, mod, l, norm_w, hnw, w_mix, ff_in, ff_out):
    b, s, d = x.shape
    tm = min(512, s)
    tok = lambda w: pl.BlockSpec((None, tm, w), lambda bi, i: (bi, i, 0))
    tr = pl.BlockSpec((None, D_MLSTM, tm), lambda bi, i: (bi, 0, i))
    hnw_b = jnp.broadcast_to(hnw.reshape(D_MLSTM, 1), (D_MLSTM, tm))
    return pl.pallas_call(
        _combine_ffn_kernel,
        grid=(b, s // tm),
        in_specs=[tok(d), tok(D_FOURIER), tr, tr, tr,
                  pl.BlockSpec((None, N_MOD, d), _mod_index(mod)),
                  _resident(norm_w.shape, (l,)), _resident(hnw_b.shape), _resident(w_mix.shape, (l,)),
                  _resident(ff_in.shape, (l, 1)), _resident(ff_out.shape, (l, 1))],
        out_specs=tok(d),
        out_shape=jax.ShapeDtypeStruct(x.shape, F32),
        scratch_shapes=[pltpu.VMEM((tm, D_FF), BF16)],
        compiler_params=_cparams(2, 60),
        name="combine_ffn",
    )(x, yf, htf, htb, zt, mod, norm_w, hnw_b, w_mix, ff_in, ff_out)


def _gate_layout(w_gate, gate_b):
    h = MLSTM_HEADS
    wt = w_gate.T
    w = jnp.concatenate([wt[0:h], wt[2 * h:3 * h], wt[h:2 * h], wt[3 * h:4 * h]], axis=0)
    bias = jnp.concatenate([gate_b[0], gate_b[2], gate_b[1], gate_b[3]])
    return w, jnp.broadcast_to(bias.reshape(GATE_ROWS, 1), (GATE_ROWS, CHUNK))


def kernel(x, c, ctx, c_ctx, w_ada, b_ada, norm_w, w_ff_in, w_ff_out, w_in, w_fmix,
           conv_w, conv_b, w_qkv, gate_b, mlstm_norm_w, w_out):
    b = x.shape[0]
    d = D_MODEL
    n_cond = b + 1
    assert n_cond <= SUBLANES
    cc = jnp.concatenate([c, c_ctx[None], jnp.zeros((SUBLANES - n_cond, d), F32)], axis=0)
    mod = _modulation(cc, w_ada, b_ada).reshape(DEPTH, SUBLANES, N_MOD, d)
    w_fold = _fold_fourier(w_in, w_fmix)
    ff_in = w_ff_in.astype(BF16)
    ff_out = w_ff_out.astype(BF16)
    w_mix = w_out.astype(BF16)

    xl, xc = x, ctx
    c_zero = jnp.zeros((b, N_HD, STATE_ROWS, MLSTM_HD), F32)
    m_zero = jnp.zeros((b, N_HD, CHUNK), F32)
    for l in range(DEPTH):
        last = l == DEPTH - 1
        mod_l = mod[l, :b]
        mod_c = mod[l, b:b + 1]
        o_xm = D_FOURIER
        o_z = D_FOURIER + D_MLSTM
        o_g = D_FOURIER + 2 * D_MLSTM
        w_gate, g_bias = _gate_layout(w_in[l, :, o_g:], gate_b[l])
        pw = (w_fold, w_in[l, :, o_xm:o_z].astype(BF16),
              jnp.concatenate([w_in[l, :, o_z:o_g].T, w_gate], axis=0).astype(BF16))
        cw = conv_w[l].reshape(9, D_MLSTM)
        cb = conv_b[l].reshape(1, D_MLSTM)
        qkvw = (jnp.swapaxes(w_qkv[l, 0], 1, 2).astype(BF16), w_qkv[l, 1].astype(BF16),
                jnp.swapaxes(w_qkv[l, 2], 1, 2).astype(BF16))
        hnw = mlstm_norm_w[l]

        xl, z_l, xm_l, zt_l, gt_l = _ffn_inproj(xl, mod_l, l, norm_w, ff_in, ff_out, *pw)
        xc, z_c, xm_c, zt_c, gt_c = _ffn_inproj(xc, mod_c, l, norm_w, ff_in, ff_out, *pw)
        qkv_l = _convqkv(xm_l, GRID_W, cw, cb, *qkvw)
        qkv_c = _convqkv(xm_c, xm_c.shape[1], cw, cb, *qkvw)
        htf_c, htb_c, c_ctx_st, m_ctx_st = _mlstm(*qkv_c, gt_c, g_bias, c_zero, m_zero)
        htf_l, htb_l, _, _ = _mlstm(*qkv_l, gt_l, g_bias, c_ctx_st, m_ctx_st)
        yf_l = _seq_dft(z_l)
        xl = _combine_ffn(xl, yf_l, htf_l, htb_l, zt_l, mod_l, l, norm_w, hnw, w_mix, ff_in, ff_out)
        if not last:
            yf_c = _dense_dft(z_c)
            xc = _combine_ffn(xc, yf_c, htf_c, htb_c, zt_c, mod_c, l, norm_w, hnw, w_mix, ff_in, ff_out)
    return xl
```

```python
import functools
import math

import numpy as np
import jax
import jax.numpy as jnp
from jax import lax
from jax.experimental import pallas as pl
from jax.experimental.pallas import tpu as pltpu

D_MODEL = 1024
DEPTH = 2
GRID_W = 64
D_FOURIER = 512
FOURIER_GROUPS = 8
FOURIER_GW = 64
D_MLSTM = 512
MLSTM_HEADS = 4
MLSTM_HD = 128
CHUNK = 128
D_FF = 2816
N_MOD = 9
FFN_RES = 0.5
EPS = 1e-6

LANES = 128
SUBLANES = 8
PACKED_SUBLANES = 16
MXU_DIM = 256
VMEM_BYTES = 64 * 1024 * 1024
N_HD = 2 * MLSTM_HEADS
STATE_ROWS = MLSTM_HD + PACKED_SUBLANES
GATE_ROWS = 2 * N_HD
SCAN_CHUNKS = 8

F32 = jnp.float32
BF16 = jnp.bfloat16
NT = (((1,), (1,)), ((), ()))
TN = (((0,), (0,)), ((), ()))


def _cparams(n_grid, vmem_mb):
    return pltpu.CompilerParams(
        dimension_semantics=("arbitrary",) * n_grid,
        vmem_limit_bytes=min(vmem_mb * 1024 * 1024, VMEM_BYTES - 4 * 1024 * 1024))


def _dot(a, b):
    return jnp.dot(a, b, preferred_element_type=F32)


def _dot_nt(a, b):
    return lax.dot_general(a, b, NT, preferred_element_type=F32)


def _rms(x, w):
    return x * lax.rsqrt(jnp.mean(x * x, axis=-1, keepdims=True) + EPS) * w


def _mod3(m, j):
    return m[3 * j:3 * j + 1], m[3 * j + 1:3 * j + 2], m[3 * j + 2:3 * j + 3]


def _mod_index(mod):
    if mod.shape[0] == 1:
        return lambda b, i: (0, 0, 0)
    return lambda b, i: (b, 0, 0)


def _full(shape):
    return pl.BlockSpec(shape, lambda *_: (0,) * len(shape))


def _resident(shape, lead=()):
    rest = tuple(shape[len(lead):])
    index = tuple(lead) + (0,) * len(rest)
    return pl.BlockSpec((None,) * len(lead) + rest, lambda *_: index, pipeline_mode=pl.Buffered(1))


def _mod_kernel(c_ref, w_ref, b_ref, o_ref):
    c = c_ref[...]
    s = (c * jax.nn.sigmoid(c)).astype(BF16)
    o_ref[...] = _dot(s, w_ref[...].astype(BF16)) + b_ref[...]


def _modulation(cc, w_ada, b_ada):
    d = D_MODEL
    return pl.pallas_call(
        _mod_kernel,
        grid=(DEPTH, N_MOD),
        in_specs=[pl.BlockSpec((SUBLANES, d), lambda l, n: (0, 0)),
                  pl.BlockSpec((None, d, d), lambda l, n: (l, 0, n)),
                  pl.BlockSpec((None, 1, d), lambda l, n: (l, 0, n))],
        out_specs=pl.BlockSpec((None, SUBLANES, d), lambda l, n: (l, 0, n)),
        out_shape=jax.ShapeDtypeStruct((DEPTH, SUBLANES, N_MOD * d), F32),
        compiler_params=_cparams(2, 32),
        name="modulation",
    )(cc, w_ada, b_ada.reshape(DEPTH, 1, N_MOD * d))


FF_CHUNK = MXU_DIM
N_FF_CHUNKS = D_FF // FF_CHUNK


def _ffn_body(x, mod, j, nw_ref, wi_ref, wo_ref, a_ref):
    shift, scale, gate = _mod3(mod, j)
    h = (_rms(x, nw_ref[2 * j:2 * j + 1]) * (1.0 + scale) + shift).astype(BF16)
    for c in range(N_FF_CHUNKS):
        sl = slice(c * FF_CHUNK, (c + 1) * FF_CHUNK)
        g = _dot(h, wi_ref[:, sl])
        u = _dot(h, wi_ref[:, D_FF + c * FF_CHUNK:D_FF + (c + 1) * FF_CHUNK])
        a_ref[:, sl] = (g * jax.nn.sigmoid(g) * u).astype(BF16)
    y = _dot(a_ref[...], wo_ref[...])
    return x + FFN_RES * gate * _rms(y, nw_ref[2 * j + 1:2 * j + 2])


def _fold_kernel(bdw_ref, bdc_ref, bds_ref, o_ref):
    hi = lax.Precision.HIGHEST
    bdw = bdw_ref[...]
    t = MXU_DIM
    for part, dft_ref in enumerate((bdc_ref, bds_ref)):
        g = jnp.dot(dft_ref[...], bdw, precision=hi, preferred_element_type=F32)
        for i in range(D_FOURIER // t):
            o_ref[part * (D_FOURIER // t) + i] = g[i * t:(i + 1) * t, i * t:(i + 1) * t].astype(BF16)


def _channel_dft_blocks():
    c = np.arange(FOURIER_GW, dtype=np.float64)
    ang = 2.0 * np.pi * np.outer(c, c) / FOURIER_GW
    eye = np.eye(FOURIER_GROUPS)
    scale = 1.0 / math.sqrt(FOURIER_GW)
    bdc = np.kron(eye, np.cos(ang) * scale)
    bds = np.kron(eye, -np.sin(ang) * scale)
    return jnp.asarray(bdc, F32), jnp.asarray(bds, F32)


N_MIX_TILES = 2 * D_FOURIER // MXU_DIM


def _fold_fourier(w_fmix):
    eye = jnp.eye(FOURIER_GROUPS, dtype=F32)
    bdw = jnp.einsum("gh,lgcd->lgchd", eye, w_fmix).reshape(DEPTH, D_FOURIER, D_FOURIER)
    bdc, bds = _channel_dft_blocks()
    sq = pl.BlockSpec((D_FOURIER, D_FOURIER), lambda l: (0, 0))
    return pl.pallas_call(
        _fold_kernel,
        grid=(DEPTH,),
        in_specs=[pl.BlockSpec((None, D_FOURIER, D_FOURIER), lambda l: (l, 0, 0)), sq, sq],
        out_specs=pl.BlockSpec((None, N_MIX_TILES, MXU_DIM, MXU_DIM), lambda l: (l, 0, 0, 0)),
        out_shape=jax.ShapeDtypeStruct((DEPTH, N_MIX_TILES, MXU_DIM, MXU_DIM), BF16),
        compiler_params=_cparams(1, 32),
        name="fold_fourier",
    )(bdw, bdc, bds)


def _ffn_inproj_kernel(x_ref, mod_ref, nw_ref, wi_ref, wo_ref, wfm_ref, g_ref, wt_ref,
                       o_ref, z_ref, xm_ref, zt_ref, gt_ref, a_ref):
    mod = mod_ref[...]
    x1 = _ffn_body(x_ref[...], mod, 0, nw_ref, wi_ref, wo_ref, a_ref)
    o_ref[...] = x1
    shift, scale, _ = _mod3(mod, 1)
    h = (_rms(x1, nw_ref[2:3]) * (1.0 + scale) + shift).astype(BF16)
    u = _dot(h, wfm_ref[...])
    xm_ref[...] = u[:, D_FOURIER:]
    xf = u[:, :D_FOURIER].astype(BF16)
    t = MXU_DIM
    n_src = D_FOURIER // t
    for i in range(N_MIX_TILES):
        src = xf[:, (i % n_src) * t:(i % n_src + 1) * t]
        z_ref[:, i * t:(i + 1) * t] = _dot(src, g_ref[i])
    tt = _dot_nt(wt_ref[...], h)
    zt_ref[...] = tt[:D_MLSTM]
    gt_ref[...] = tt[D_MLSTM:]


def _ffn_inproj(x, mod, l, norm_w, ff_in, ff_out, wfm, g_mix, wt):
    b, s, d = x.shape
    tm = min(512, s)
    tok = lambda w: pl.BlockSpec((None, tm, w), lambda bi, i: (bi, i, 0))
    tr = lambda r: pl.BlockSpec((None, r, tm), lambda bi, i: (bi, 0, i))
    return pl.pallas_call(
        _ffn_inproj_kernel,
        grid=(b, s // tm),
        in_specs=[tok(d), pl.BlockSpec((None, N_MOD, d), _mod_index(mod)), _resident(norm_w.shape, (l,)),
                  _resident(ff_in.shape, (l, 0)), _resident(ff_out.shape, (l, 0)),
                  _resident(wfm.shape, (l,)), _resident(g_mix.shape, (l,)), _resident(wt.shape)],
        out_specs=[tok(d), tok(2 * D_FOURIER), tok(D_MLSTM), tr(D_MLSTM), tr(GATE_ROWS)],
        out_shape=[jax.ShapeDtypeStruct(x.shape, F32),
                   jax.ShapeDtypeStruct((b, s, 2 * D_FOURIER), F32),
                   jax.ShapeDtypeStruct((b, s, D_MLSTM), F32),
                   jax.ShapeDtypeStruct((b, D_MLSTM, s), F32),
                   jax.ShapeDtypeStruct((b, GATE_ROWS, s), F32)],
        scratch_shapes=[pltpu.VMEM((tm, D_FF), BF16)],
        compiler_params=_cparams(2, 60),
        name="ffn_inproj",
    )(x, mod, norm_w, ff_in, ff_out, wfm, g_mix, wt)


def _convqkv_kernel(width, rows, halo, prev_ref, cur_ref, next_ref, cw_ref, cb_ref,
                    wqt_ref, wk_ref, wvt_ref, qt_ref, k_ref, vt_ref):
    i = pl.program_id(1)
    tm = cur_ref.shape[0]
    cur = cur_ref[...]
    prev = jnp.where(i > 0, prev_ref[...], 0.0)
    nxt = jnp.where(i < pl.num_programs(1) - 1, next_ref[...], 0.0)
    ext = jnp.concatenate([prev, cur, nxt], axis=0)
    n = ext.shape[0]
    r = lax.broadcasted_iota(jnp.int32, ext.shape, 0)
    col = (r + (width - halo % width)) & (width - 1)
    left = jnp.where(col == 0, 0.0, pltpu.roll(ext, 1, axis=0))
    right = jnp.where(col == width - 1, 0.0, pltpu.roll(ext, n - 1, axis=0))
    taps = (left, ext, right)
    cw = cw_ref[...]
    acc = jnp.zeros((tm, cur.shape[1]), F32) + cb_ref[...]
    for dy in (-1, 0, 1):
        if rows == 1 and dy != 0:
            continue
        lo = halo + dy * width
        for dx in (-1, 0, 1):
            k = (dy + 1) * 3 + (dx + 1)
            acc = acc + cw[k:k + 1] * taps[dx + 1][lo:lo + tm]
    cv = (acc * jax.nn.sigmoid(acc)).astype(BF16)
    xm = cur.astype(BF16)
    hd = MLSTM_HD
    for h in range(MLSTM_HEADS):
        sl = slice(h * hd, (h + 1) * hd)
        qt_ref[sl, :] = _dot_nt(wqt_ref[h], cv[:, sl]).astype(BF16)
        k_ref[:, sl] = (_dot(cv[:, sl], wk_ref[h]) * hd ** -0.5).astype(BF16)
        vt_ref[sl, :] = _dot_nt(wvt_ref[h], xm[:, sl]).astype(BF16)


def _convqkv(xm, width, cw, cb, wqt, wk, wvt):
    b, s, c = xm.shape
    rows = s // width
    tm = min(512, s)
    halo = LANES if rows > 1 else SUBLANES
    assert tm % width == 0 and tm % halo == 0 and (rows == 1 or halo > width)
    assert width & (width - 1) == 0
    per = tm // halo
    nblk = s // halo
    kern = functools.partial(_convqkv_kernel, width, rows, halo)
    tr = pl.BlockSpec((None, c, tm), lambda bi, i: (bi, 0, i))
    return pl.pallas_call(
        kern,
        grid=(b, s // tm),
        in_specs=[pl.BlockSpec((None, halo, c), lambda bi, i: (bi, jnp.maximum(i * per - 1, 0), 0)),
                  pl.BlockSpec((None, tm, c), lambda bi, i: (bi, i, 0)),
                  pl.BlockSpec((None, halo, c), lambda bi, i: (bi, jnp.minimum((i + 1) * per, nblk - 1), 0)),
                  _full(cw.shape), _full(cb.shape), _full(wqt.shape), _full(wk.shape), _full(wvt.shape)],
        out_specs=[tr, pl.BlockSpec((None, tm, c), lambda bi, i: (bi, i, 0)), tr],
        out_shape=[jax.ShapeDtypeStruct((b, c, s), BF16),
                   jax.ShapeDtypeStruct((b, s, c), BF16),
                   jax.ShapeDtypeStruct((b, c, s), BF16)],
        compiler_params=_cparams(2, 32),
        name="convqkv",
    )(xm, xm, xm, cw, cb, wqt, wk, wvt)


def _tri_ones():
    u = np.arange(CHUNK)[:, None]
    t = np.arange(CHUNK)[None, :]
    return jnp.asarray(np.concatenate([u <= t, u >= t], axis=0), BF16)


def _split_dot(x, w):
    hi = x.astype(BF16)
    r1 = x - hi.astype(F32)
    mid = r1.astype(BF16)
    lo = (r1 - mid.astype(F32)).astype(BF16)
    return _dot(hi, w) + _dot(mid, w) + _dot(lo, w)


def _mlstm_kernel(qtf_ref, kf_ref, vtf_ref, gtf_ref, qtb_ref, kb_ref, vtb_ref, gtb_ref,
                  bias_ref, tri_ref, c0_ref, m0_ref, htf_ref, htb_ref, cfin_ref, mfin_ref, c_st, m_st):
    j = pl.program_id(1)
    L = CHUNK
    hd = MLSTM_HD
    n_sub = gtf_ref.shape[1] // L

    @pl.when(j == 0)
    def _():
        c_st[...] = c0_ref[...]
        m_st[...] = m0_ref[...]

    row_fwd = lax.broadcasted_iota(jnp.int32, (N_HD, L), 0) < MLSTM_HEADS
    src_row = lax.broadcasted_iota(jnp.int32, (L, L), 0)
    out_col = lax.broadcasted_iota(jnp.int32, (L, L), 1)
    one_row = jnp.where(lax.broadcasted_iota(jnp.int32, (PACKED_SUBLANES, L), 0) == 0, 1.0, 0.0).astype(BF16)
    bcast = lambda col: jnp.broadcast_to(col, (N_HD, L))

    def operands(u, p):
        d, hh = divmod(p, MLSTM_HEADS)
        hs = slice(hh * hd, (hh + 1) * hd)
        if d == 0:
            tl = slice(u * L, (u + 1) * L)
            return qtf_ref[hs, tl], kf_ref[tl, hs], vtf_ref[hs, tl], htf_ref, hs, tl
        tl = slice((n_sub - 1 - u) * L, (n_sub - u) * L)
        return qtb_ref[hs, tl], kb_ref[tl, hs], vtb_ref[hs, tl], htb_ref, hs, tl

    gates = []
    for u in range(n_sub):
        fl = slice(u * L, (u + 1) * L)
        bl = slice((n_sub - 1 - u) * L, (n_sub - u) * L)
        ig = jnp.where(row_fwd, gtf_ref[:N_HD, fl], gtb_ref[:N_HD, bl]) + bias_ref[:N_HD]
        fp = jnp.where(row_fwd, gtf_ref[N_HD:, fl], gtb_ref[N_HD:, bl]) + bias_ref[N_HD:]
        lf = jax.nn.log_sigmoid(fp)
        lf2 = jnp.concatenate([jnp.where(row_fwd, lf, 0.0), jnp.where(row_fwd, 0.0, lf)], axis=1)
        bc = _split_dot(lf2, tri_ref[...])
        gs = ig - bc
        btot = jnp.where(row_fwd, bcast(bc[:, L - 1:L]), bcast(bc[:, 0:1]))
        gmax = bcast(jnp.max(gs, axis=1, keepdims=True))
        gates.append((bc, gs, btot, gmax, jnp.exp(gs - gmax)))

    scores, local = {}, {}
    for u in range(n_sub):
        w_st = gates[u][4]
        for p in range(N_HD):
            qt, k, vt, _, _, _ = operands(u, p)
            scores[u, p] = _dot(k, qt)
            vaug = jnp.concatenate([vt, one_row], axis=0)
            vw = (vaug.astype(F32) * w_st[p:p + 1, :]).astype(BF16)
            local[u, p] = _dot(vw, k)

    for u in range(n_sub):
        bc, gs, btot, gmax, _ = gates[u]
        m_prev = m_st[...]
        m_loc = btot + gmax
        m_new = jnp.maximum(btot + m_prev, m_loc)
        dec = jnp.exp(btot + m_prev - m_new)
        inj = jnp.exp(m_loc - m_new)
        m_st[...] = m_new
        for p in range(N_HD):
            qt, k, vt, dst, hs, tl = operands(u, p)
            seen = (src_row <= out_col) if p < MLSTM_HEADS else (src_row >= out_col)
            gs_seen = jnp.where(seen, jnp.broadcast_to(gs[p:p + 1, :], (L, L)).T, -jnp.inf)
            xm = jnp.maximum(m_prev[p:p + 1, :], jnp.max(gs_seen, axis=0, keepdims=True))
            pt = (scores[u, p] * jnp.exp(gs_seen - xm)).astype(BF16)
            qw = (qt.astype(F32) * jnp.exp(m_prev[p:p + 1, :] - xm)).astype(BF16)
            vaug = jnp.concatenate([vt, one_row], axis=0)
            c_prev = c_st[p]
            lhs = jnp.concatenate([vaug, c_prev.astype(BF16)], axis=1)
            rhs = jnp.concatenate([pt, qw], axis=0)
            o = _dot(lhs, rhs)
            den = jnp.maximum(jnp.abs(o[hd:hd + 1]), jnp.exp(-(bc[p:p + 1, :] + xm)))
            dst[hs, tl] = o[:hd] * (1.0 / den)
            c_st[p] = dec[p:p + 1, :] * c_prev + inj[p:p + 1, :] * local[u, p]

    @pl.when(j == pl.num_programs(1) - 1)
    def _():
        cfin_ref[...] = c_st[...]
        mfin_ref[...] = m_st[...]


def _mlstm(qt, k, vt, gt, bias, c0, m0):
    b, c, s = qt.shape
    tb = min(SCAN_CHUNKS * CHUNK, s)
    nb = s // tb
    tri = _tri_ones()
    fwd_t = lambda bi, j: (bi, 0, j)
    bwd_t = lambda bi, j: (bi, 0, nb - 1 - j)
    fwd_n = lambda bi, j: (bi, j, 0)
    bwd_n = lambda bi, j: (bi, nb - 1 - j, 0)
    st_c = pl.BlockSpec((None, N_HD, STATE_ROWS, MLSTM_HD), lambda bi, j: (bi, 0, 0, 0))
    st_m = pl.BlockSpec((None, N_HD, CHUNK), lambda bi, j: (bi, 0, 0))
    side = lambda ti, ni: [pl.BlockSpec((None, c, tb), ti), pl.BlockSpec((None, tb, c), ni),
                           pl.BlockSpec((None, c, tb), ti), pl.BlockSpec((None, GATE_ROWS, tb), ti)]
    return pl.pallas_call(
        _mlstm_kernel,
        grid=(b, nb),
        in_specs=side(fwd_t, fwd_n) + side(bwd_t, bwd_n) + [_full(bias.shape), _full(tri.shape), st_c, st_m],
        out_specs=[pl.BlockSpec((None, c, tb), fwd_t), pl.BlockSpec((None, c, tb), bwd_t), st_c, st_m],
        out_shape=[jax.ShapeDtypeStruct((b, c, s), F32),
                   jax.ShapeDtypeStruct((b, c, s), F32),
                   jax.ShapeDtypeStruct(c0.shape, F32),
                   jax.ShapeDtypeStruct(m0.shape, F32)],
        scratch_shapes=[pltpu.VMEM((N_HD, STATE_ROWS, MLSTM_HD), F32),
                        pltpu.VMEM((N_HD, CHUNK), F32)],
        compiler_params=_cparams(2, 56),
        name="mlstm",
    )(qt, k, vt, gt, qt, k, vt, gt, bias, tri, c0, m0)


DFT_GROUP = SUBLANES


def _dft_tables(s, n2):
    n1 = s // n2
    k1 = np.arange(n1, dtype=np.float64)
    s1 = np.arange(n1, dtype=np.float64)
    ma = np.zeros((n2, 2 * n1, 2 * n1), np.float64)
    for s2 in range(n2):
        th = 2.0 * np.pi * np.outer(k1, s1 * n2 + s2) / s
        c, sn = np.cos(th), np.sin(th)
        ma[s2] = np.block([[c, sn], [-sn, c]])
    k2 = np.arange(n2, dtype=np.float64)
    ph = 2.0 * np.pi * np.outer(k2, k2) / n2
    eye = np.eye(DFT_GROUP)
    scale = 1.0 / math.sqrt(s)
    lb = np.einsum("ksr,ab->kasrb", np.stack([np.cos(ph), np.sin(ph)], axis=-1) * scale, eye)
    lb = lb.reshape(n2 * DFT_GROUP, n2 * 2 * DFT_GROUP)
    return jnp.asarray(ma, F32).astype(BF16), jnp.asarray(lb, F32).astype(BF16)


def _seq_dft_kernel(n_a, z_ref, ma_ref, lb_ref, y_ref, t_scr):
    j = pl.program_id(1)
    g = DFT_GROUP
    n2 = t_scr.shape[0]
    n1 = t_scr.shape[2]

    @pl.when(j < n_a)
    def _():
        for s in range(g):
            zs = z_ref[:, s, :]
            rhs = jnp.concatenate([zs[:, :D_FOURIER], zs[:, D_FOURIER:]], axis=0).astype(BF16)
            t_scr[j * g + s] = _dot(ma_ref[s], rhs).reshape(2, n1, D_FOURIER)

    @pl.when(j >= n_a)
    def _():
        for q in range(y_ref.shape[1] // g):
            k0 = pl.multiple_of((j - n_a) * y_ref.shape[1] + q * g, g)
            rhs = t_scr[:, :, pl.ds(k0, g), :].reshape(n2 * 2 * g, D_FOURIER).astype(BF16)
            y_ref[:, q * g:(q + 1) * g, :] = _dot(lb_ref[...], rhs).reshape(n2, g, D_FOURIER)


def _seq_dft(z, n2=64):
    b, s, w = z.shape
    n1 = s // n2
    g = DFT_GROUP
    n_a = n2 // g
    gb = 2 * g
    n_b = n1 // gb
    ma, lb = _dft_tables(s, n2)
    y = pl.pallas_call(
        functools.partial(_seq_dft_kernel, n_a),
        grid=(b, n_a + n_b),
        in_specs=[pl.BlockSpec((None, n1, g, w), lambda bi, j: (bi, 0, jnp.minimum(j, n_a - 1), 0)),
                  pl.BlockSpec((g, 2 * n1, 2 * n1), lambda bi, j: (jnp.minimum(j, n_a - 1), 0, 0)),
                  _full(lb.shape)],
        out_specs=pl.BlockSpec((None, n2, gb, D_FOURIER), lambda bi, j: (bi, 0, jnp.maximum(j - n_a, 0), 0)),
        out_shape=jax.ShapeDtypeStruct((b, n2, n1, D_FOURIER), F32),
        scratch_shapes=[pltpu.VMEM((n2, 2, n1, D_FOURIER), F32)],
        compiler_params=_cparams(2, 60),
        name="seq_dft",
    )(z.reshape(b, n1, n2, w), ma, lb)
    return y.reshape(b, s, D_FOURIER)


def _dense_dft_kernel(z_ref, ld_ref, y_ref):
    rhs = jnp.concatenate([z_ref[:, :D_FOURIER], z_ref[:, D_FOURIER:]], axis=0).astype(BF16)
    y_ref[...] = _dot(ld_ref[...], rhs)


def _dense_dft(z):
    b, s, w = z.shape
    k = np.arange(s, dtype=np.float64)
    th = 2.0 * np.pi * np.outer(k, k) / s
    ld = jnp.asarray(np.concatenate([np.cos(th), np.sin(th)], axis=1) / math.sqrt(s), F32).astype(BF16)
    return pl.pallas_call(
        _dense_dft_kernel,
        grid=(b,),
        in_specs=[pl.BlockSpec((None, s, w), lambda bi: (bi, 0, 0)), _full(ld.shape)],
        out_specs=pl.BlockSpec((None, s, D_FOURIER), lambda bi: (bi, 0, 0)),
        out_shape=jax.ShapeDtypeStruct((b, s, D_FOURIER), F32),
        compiler_params=_cparams(1, 32),
        name="dft_dense",
    )(z, ld)


def _combine_ffn_kernel(x_ref, yf_ref, htf_ref, htb_ref, zt_ref, mod_ref, nw_ref, hnw_ref, wom_ref,
                        wi_ref, wo_ref, o_ref, a_ref):
    mod = mod_ref[...]
    _, _, gate = _mod3(mod, 1)
    hd = MLSTM_HD
    hs = htf_ref[...] + htb_ref[...]
    parts = []
    for h in range(MLSTM_HEADS):
        seg = hs[h * hd:(h + 1) * hd]
        mu = jnp.mean(seg, axis=0, keepdims=True)
        cen = seg - mu
        var = jnp.mean(cen * cen, axis=0, keepdims=True)
        parts.append(cen * lax.rsqrt(var + EPS))
    hn = jnp.concatenate(parts, axis=0) * hnw_ref[...]
    ymt = (hn * jax.nn.sigmoid(zt_ref[...])).astype(BF16)
    y = (_dot(yf_ref[...].astype(BF16), wom_ref[:D_FOURIER])
         + lax.dot_general(ymt, wom_ref[D_FOURIER:], TN, preferred_element_type=F32))
    x2 = x_ref[...] + gate * _rms(y, nw_ref[3:4])
    o_ref[...] = _ffn_body(x2, mod, 2, nw_ref, wi_ref, wo_ref, a_ref)


def _combine_ffn(x, yf, htf, htb, zt, mod, l, norm_w, hnw, w_mix, ff_in, ff_out):
    b, s, d = x.shape
    tm = min(512, s)
    tok = lambda w: pl.BlockSpec((None, tm, w), lambda bi, i: (bi, i, 0))
    tr = pl.BlockSpec((None, D_MLSTM, tm), lambda bi, i: (bi, 0, i))
    hnw_b = jnp.broadcast_to(hnw.reshape(D_MLSTM, 1), (D_MLSTM, tm))
    return pl.pallas_call(
        _combine_ffn_kernel,
        grid=(b, s // tm),
        in_specs=[tok(d), tok(D_FOURIER), tr, tr, tr,
                  pl.BlockSpec((None, N_MOD, d), _mod_index(mod)),
                  _resident(norm_w.shape, (l,)), _resident(hnw_b.shape), _resident(w_mix.shape, (l,)),
                  _resident(ff_in.shape, (l, 1)), _resident(ff_out.shape, (l, 1))],
        out_specs=tok(d),
        out_shape=jax.ShapeDtypeStruct(x.shape, F32),
        scratch_shapes=[pltpu.VMEM((tm, D_FF), BF16)],
        compiler_params=_cparams(2, 60),
        name="combine_ffn",
    )(x, yf, htf, htb, zt, mod, norm_w, hnw_b, w_mix, ff_in, ff_out)


def _gate_layout(w_gate, gate_b):
    h = MLSTM_HEADS
    wt = w_gate.T
    w = jnp.concatenate([wt[0:h], wt[2 * h:3 * h], wt[h:2 * h], wt[3 * h:4 * h]], axis=0)
    bias = jnp.concatenate([gate_b[0], gate_b[2], gate_b[1], gate_b[3]])
    return w, jnp.broadcast_to(bias.reshape(GATE_ROWS, 1), (GATE_ROWS, CHUNK))


def kernel(x, c, ctx, c_ctx, w_ada, b_ada, norm_w, w_ff_in, w_ff_out, w_in, w_fmix,
           conv_w, conv_b, w_qkv, gate_b, mlstm_norm_w, w_out):
    b = x.shape[0]
    d = D_MODEL
    n_cond = b + 1
    assert n_cond <= SUBLANES
    cc = jnp.concatenate([c, c_ctx[None], jnp.zeros((SUBLANES - n_cond, d), F32)], axis=0)
    mod = _modulation(cc, w_ada, b_ada).reshape(DEPTH, SUBLANES, N_MOD, d)
    g_mix = _fold_fourier(w_fmix)
    w_fm = w_in[:, :, :D_FOURIER + D_MLSTM].astype(BF16)
    ff_in = w_ff_in.astype(BF16)
    ff_out = w_ff_out.astype(BF16)
    w_mix = w_out.astype(BF16)

    xl, xc = x, ctx
    c_zero = jnp.zeros((b, N_HD, STATE_ROWS, MLSTM_HD), F32)
    m_zero = jnp.zeros((b, N_HD, CHUNK), F32)
    for l in range(DEPTH):
        last = l == DEPTH - 1
        mod_l = mod[l, :b]
        mod_c = mod[l, b:b + 1]
        o_z = D_FOURIER + D_MLSTM
        o_g = D_FOURIER + 2 * D_MLSTM
        w_gate, g_bias = _gate_layout(w_in[l, :, o_g:], gate_b[l])
        pw = (w_fm, g_mix, jnp.concatenate([w_in[l, :, o_z:o_g].T, w_gate], axis=0).astype(BF16))
        cw = conv_w[l].reshape(9, D_MLSTM)
        cb = conv_b[l].reshape(1, D_MLSTM)
        qkvw = (jnp.swapaxes(w_qkv[l, 0], 1, 2).astype(BF16), w_qkv[l, 1].astype(BF16),
                jnp.swapaxes(w_qkv[l, 2], 1, 2).astype(BF16))
        hnw = mlstm_norm_w[l]

        xl, z_l, xm_l, zt_l, gt_l = _ffn_inproj(xl, mod_l, l, norm_w, ff_in, ff_out, *pw)
        xc, z_c, xm_c, zt_c, gt_c = _ffn_inproj(xc, mod_c, l, norm_w, ff_in, ff_out, *pw)
        qkv_l = _convqkv(xm_l, GRID_W, cw, cb, *qkvw)
        qkv_c = _convqkv(xm_c, xm_c.shape[1], cw, cb, *qkvw)
        htf_c, htb_c, c_ctx_st, m_ctx_st = _mlstm(*qkv_c, gt_c, g_bias, c_zero, m_zero)
        htf_l, htb_l, _, _ = _mlstm(*qkv_l, gt_l, g_bias, c_ctx_st, m_ctx_st)
        yf_l = _seq_dft(z_l)
        xl = _combine_ffn(xl, yf_l, htf_l, htb_l, zt_l, mod_l, l, norm_w, hnw, w_mix, ff_in, ff_out)
        if not last:
            yf_c = _dense_dft(z_c)
            xc = _combine_ffn(xc, yf_c, htf_c, htb_c, zt_c, mod_c, l, norm_w, hnw, w_mix, ff_in, ff_out)
    return xl
```

```python
import functools
import math

import numpy as np
import jax
import jax.numpy as jnp
from jax import lax
from jax.experimental import pallas as pl
from jax.experimental.pallas import tpu as pltpu

D_MODEL = 1024
DEPTH = 2
GRID_W = 64
D_FOURIER = 512
FOURIER_GROUPS = 8
FOURIER_GW = 64
D_MLSTM = 512
MLSTM_HEADS = 4
MLSTM_HD = 128
CHUNK = 128
D_FF = 2816
N_MOD = 9
FFN_RES = 0.5
EPS = 1e-6

LANES = 128
SUBLANES = 8
PACKED_SUBLANES = 16
MXU_DIM = 256
VMEM_BYTES = 64 * 1024 * 1024
N_HD = 2 * MLSTM_HEADS
STATE_ROWS = MLSTM_HD + PACKED_SUBLANES
GATE_ROWS = 2 * N_HD
SCAN_CHUNKS = 8

F32 = jnp.float32
BF16 = jnp.bfloat16
NT = (((1,), (1,)), ((), ()))
TN = (((0,), (0,)), ((), ()))


def _cparams(n_grid, vmem_mb):
    return pltpu.CompilerParams(
        dimension_semantics=("arbitrary",) * n_grid,
        vmem_limit_bytes=min(vmem_mb * 1024 * 1024, VMEM_BYTES - 4 * 1024 * 1024))


def _dot(a, b):
    return jnp.dot(a, b, preferred_element_type=F32)


def _dot_nt(a, b):
    return lax.dot_general(a, b, NT, preferred_element_type=F32)


def _rms(x, w):
    return x * lax.rsqrt(jnp.mean(x * x, axis=-1, keepdims=True) + EPS) * w


def _mod3(m, j):
    return m[3 * j:3 * j + 1], m[3 * j + 1:3 * j + 2], m[3 * j + 2:3 * j + 3]


def _mod_index(mod):
    if mod.shape[0] == 1:
        return lambda b, i: (0, 0, 0)
    return lambda b, i: (b, 0, 0)


def _full(shape):
    return pl.BlockSpec(shape, lambda *_: (0,) * len(shape))


def _resident(shape, lead=()):
    rest = tuple(shape[len(lead):])
    index = tuple(lead) + (0,) * len(rest)
    return pl.BlockSpec((None,) * len(lead) + rest, lambda *_: index, pipeline_mode=pl.Buffered(1))


def _mod_kernel(c_ref, w_ref, b_ref, o_ref):
    c = c_ref[...]
    s = (c * jax.nn.sigmoid(c)).astype(BF16)
    o_ref[...] = _dot(s, w_ref[...].astype(BF16)) + b_ref[...]


def _modulation(cc, w_ada, b_ada):
    d = D_MODEL
    return pl.pallas_call(
        _mod_kernel,
        grid=(DEPTH, N_MOD),
        in_specs=[pl.BlockSpec((SUBLANES, d), lambda l, n: (0, 0)),
                  pl.BlockSpec((None, d, d), lambda l, n: (l, 0, n)),
                  pl.BlockSpec((None, 1, d), lambda l, n: (l, 0, n))],
        out_specs=pl.BlockSpec((None, SUBLANES, d), lambda l, n: (l, 0, n)),
        out_shape=jax.ShapeDtypeStruct((DEPTH, SUBLANES, N_MOD * d), F32),
        compiler_params=_cparams(2, 32),
        name="modulation",
    )(cc, w_ada, b_ada.reshape(DEPTH, 1, N_MOD * d))


FF_CHUNK = MXU_DIM
N_FF_CHUNKS = D_FF // FF_CHUNK


def _ffn_prenorm(x, mod, j, nw_ref):
    shift, scale, _ = _mod3(mod, j)
    return (_rms(x, nw_ref[2 * j:2 * j + 1]) * (1.0 + scale) + shift).astype(BF16)


def _ffn_chunks(h, wi_ref, a_ref, lo, hi):
    for c in range(lo, hi):
        sl = slice(c * FF_CHUNK, (c + 1) * FF_CHUNK)
        g = _dot(h, wi_ref[:, sl])
        u = _dot(h, wi_ref[:, D_FF + c * FF_CHUNK:D_FF + (c + 1) * FF_CHUNK])
        a_ref[:, sl] = (g * jax.nn.sigmoid(g) * u).astype(BF16)


def _ffn_finish(x, mod, j, nw_ref, wo_ref, a_ref):
    _, _, gate = _mod3(mod, j)
    y = _dot(a_ref[...], wo_ref[...])
    return x + FFN_RES * gate * _rms(y, nw_ref[2 * j + 1:2 * j + 2])


def _ffn_body(x, mod, j, nw_ref, wi_ref, wo_ref, a_ref):
    _ffn_chunks(_ffn_prenorm(x, mod, j, nw_ref), wi_ref, a_ref, 0, N_FF_CHUNKS)
    return _ffn_finish(x, mod, j, nw_ref, wo_ref, a_ref)


def _fold_kernel(bdw_ref, bdc_ref, bds_ref, o_ref):
    hi = lax.Precision.HIGHEST
    bdw = bdw_ref[...]
    t = MXU_DIM
    for part, dft_ref in enumerate((bdc_ref, bds_ref)):
        g = jnp.dot(dft_ref[...], bdw, precision=hi, preferred_element_type=F32)
        for i in range(D_FOURIER // t):
            o_ref[part * (D_FOURIER // t) + i] = g[i * t:(i + 1) * t, i * t:(i + 1) * t].astype(BF16)


def _channel_dft_blocks():
    c = np.arange(FOURIER_GW, dtype=np.float64)
    ang = 2.0 * np.pi * np.outer(c, c) / FOURIER_GW
    eye = np.eye(FOURIER_GROUPS)
    scale = 1.0 / math.sqrt(FOURIER_GW)
    bdc = np.kron(eye, np.cos(ang) * scale)
    bds = np.kron(eye, -np.sin(ang) * scale)
    return jnp.asarray(bdc, F32), jnp.asarray(bds, F32)


N_MIX_TILES = 2 * D_FOURIER // MXU_DIM


def _fold_fourier(w_fmix):
    eye = jnp.eye(FOURIER_GROUPS, dtype=F32)
    bdw = jnp.einsum("gh,lgcd->lgchd", eye, w_fmix).reshape(DEPTH, D_FOURIER, D_FOURIER)
    bdc, bds = _channel_dft_blocks()
    sq = pl.BlockSpec((D_FOURIER, D_FOURIER), lambda l: (0, 0))
    return pl.pallas_call(
        _fold_kernel,
        grid=(DEPTH,),
        in_specs=[pl.BlockSpec((None, D_FOURIER, D_FOURIER), lambda l: (l, 0, 0)), sq, sq],
        out_specs=pl.BlockSpec((None, N_MIX_TILES, MXU_DIM, MXU_DIM), lambda l: (l, 0, 0, 0)),
        out_shape=jax.ShapeDtypeStruct((DEPTH, N_MIX_TILES, MXU_DIM, MXU_DIM), BF16),
        compiler_params=_cparams(1, 32),
        name="fold_fourier",
    )(bdw, bdc, bds)


def _ffn_inproj_kernel(x_ref, mod_ref, nw_ref, wi_ref, wo_ref, wfm_ref, g_ref, wt_ref,
                       o_ref, z_ref, xm_ref, zt_ref, gt_ref, a_ref):
    mod = mod_ref[...]
    x1 = _ffn_body(x_ref[...], mod, 0, nw_ref, wi_ref, wo_ref, a_ref)
    o_ref[...] = x1
    shift, scale, _ = _mod3(mod, 1)
    h = (_rms(x1, nw_ref[2:3]) * (1.0 + scale) + shift).astype(BF16)
    u = _dot(h, wfm_ref[...])
    xm_ref[...] = u[:, D_FOURIER:]
    xf = u[:, :D_FOURIER].astype(BF16)
    t = MXU_DIM
    n_src = D_FOURIER // t
    for i in range(N_MIX_TILES):
        src = xf[:, (i % n_src) * t:(i % n_src + 1) * t]
        z_ref[:, i * t:(i + 1) * t] = _dot(src, g_ref[i])
    tt = _dot_nt(wt_ref[...], h)
    zt_ref[...] = tt[:D_MLSTM]
    gt_ref[...] = tt[D_MLSTM:]


def _ffn_inproj(x, mod, l, norm_w, ff_in, ff_out, wfm, g_mix, wt):
    b, s, d = x.shape
    tm = min(512, s)
    tok = lambda w: pl.BlockSpec((None, tm, w), lambda bi, i: (bi, i, 0))
    tr = lambda r: pl.BlockSpec((None, r, tm), lambda bi, i: (bi, 0, i))
    return pl.pallas_call(
        _ffn_inproj_kernel,
        grid=(b, s // tm),
        in_specs=[tok(d), pl.BlockSpec((None, N_MOD, d), _mod_index(mod)), _resident(norm_w.shape, (l,)),
                  _resident(ff_in.shape, (l, 0)), _resident(ff_out.shape, (l, 0)),
                  _resident(wfm.shape, (l,)), _resident(g_mix.shape, (l,)), _resident(wt.shape)],
        out_specs=[tok(d), tok(2 * D_FOURIER), tok(D_MLSTM), tr(D_MLSTM), tr(GATE_ROWS)],
        out_shape=[jax.ShapeDtypeStruct(x.shape, F32),
                   jax.ShapeDtypeStruct((b, s, 2 * D_FOURIER), F32),
                   jax.ShapeDtypeStruct((b, s, D_MLSTM), F32),
                   jax.ShapeDtypeStruct((b, D_MLSTM, s), F32),
                   jax.ShapeDtypeStruct((b, GATE_ROWS, s), F32)],
        scratch_shapes=[pltpu.VMEM((tm, D_FF), BF16)],
        compiler_params=_cparams(2, 60),
        name="ffn_inproj",
    )(x, mod, norm_w, ff_in, ff_out, wfm, g_mix, wt)


def _convqkv_kernel(width, rows, halo, prev_ref, cur_ref, next_ref, cw_ref, cb_ref,
                    wqt_ref, wk_ref, wvt_ref, qt_ref, k_ref, vt_ref):
    i = pl.program_id(1)
    tm = cur_ref.shape[0]
    cur = cur_ref[...]
    prev = jnp.where(i > 0, prev_ref[...], 0.0)
    nxt = jnp.where(i < pl.num_programs(1) - 1, next_ref[...], 0.0)
    ext = jnp.concatenate([prev, cur, nxt], axis=0)
    n = ext.shape[0]
    r = lax.broadcasted_iota(jnp.int32, ext.shape, 0)
    col = (r + (width - halo % width)) & (width - 1)
    left = jnp.where(col == 0, 0.0, pltpu.roll(ext, 1, axis=0))
    right = jnp.where(col == width - 1, 0.0, pltpu.roll(ext, n - 1, axis=0))
    taps = (left, ext, right)
    cw = cw_ref[...]
    acc = jnp.zeros((tm, cur.shape[1]), F32) + cb_ref[...]
    for dy in (-1, 0, 1):
        if rows == 1 and dy != 0:
            continue
        lo = halo + dy * width
        for dx in (-1, 0, 1):
            k = (dy + 1) * 3 + (dx + 1)
            acc = acc + cw[k:k + 1] * taps[dx + 1][lo:lo + tm]
    cv = (acc * jax.nn.sigmoid(acc)).astype(BF16)
    xm = cur.astype(BF16)
    hd = MLSTM_HD
    for h in range(MLSTM_HEADS):
        sl = slice(h * hd, (h + 1) * hd)
        qt_ref[sl, :] = _dot_nt(wqt_ref[h], cv[:, sl]).astype(BF16)
        k_ref[:, sl] = (_dot(cv[:, sl], wk_ref[h]) * hd ** -0.5).astype(BF16)
        vt_ref[sl, :] = _dot_nt(wvt_ref[h], xm[:, sl]).astype(BF16)


def _convqkv(xm, width, cw, cb, wqt, wk, wvt):
    b, s, c = xm.shape
    rows = s // width
    tm = min(512, s)
    halo = LANES if rows > 1 else SUBLANES
    assert tm % width == 0 and tm % halo == 0 and (rows == 1 or halo > width)
    assert width & (width - 1) == 0
    per = tm // halo
    nblk = s // halo
    kern = functools.partial(_convqkv_kernel, width, rows, halo)
    tr = pl.BlockSpec((None, c, tm), lambda bi, i: (bi, 0, i))
    return pl.pallas_call(
        kern,
        grid=(b, s // tm),
        in_specs=[pl.BlockSpec((None, halo, c), lambda bi, i: (bi, jnp.maximum(i * per - 1, 0), 0)),
                  pl.BlockSpec((None, tm, c), lambda bi, i: (bi, i, 0)),
                  pl.BlockSpec((None, halo, c), lambda bi, i: (bi, jnp.minimum((i + 1) * per, nblk - 1), 0)),
                  _full(cw.shape), _full(cb.shape), _full(wqt.shape), _full(wk.shape), _full(wvt.shape)],
        out_specs=[tr, pl.BlockSpec((None, tm, c), lambda bi, i: (bi, i, 0)), tr],
        out_shape=[jax.ShapeDtypeStruct((b, c, s), BF16),
                   jax.ShapeDtypeStruct((b, s, c), BF16),
                   jax.ShapeDtypeStruct((b, c, s), BF16)],
        compiler_params=_cparams(2, 32),
        name="convqkv",
    )(xm, xm, xm, cw, cb, wqt, wk, wvt)


def _tri_ones():
    u = np.arange(CHUNK)[:, None]
    t = np.arange(CHUNK)[None, :]
    return jnp.asarray(np.concatenate([u <= t, u >= t], axis=0), BF16)


def _split_dot(x, w):
    hi = x.astype(BF16)
    r1 = x - hi.astype(F32)
    mid = r1.astype(BF16)
    lo = (r1 - mid.astype(F32)).astype(BF16)
    return _dot(hi, w) + _dot(mid, w) + _dot(lo, w)


def _mlstm_kernel(qtf_ref, kf_ref, vtf_ref, gtf_ref, qtb_ref, kb_ref, vtb_ref, gtb_ref,
                  bias_ref, tri_ref, c0_ref, m0_ref, htf_ref, htb_ref, cfin_ref, mfin_ref, c_st, m_st):
    j = pl.program_id(1)
    L = CHUNK
    hd = MLSTM_HD
    n_sub = gtf_ref.shape[1] // L

    @pl.when(j == 0)
    def _():
        c_st[...] = c0_ref[...]
        m_st[...] = m0_ref[...]

    row_fwd = lax.broadcasted_iota(jnp.int32, (N_HD, L), 0) < MLSTM_HEADS
    src_row = lax.broadcasted_iota(jnp.int32, (L, L), 0)
    out_col = lax.broadcasted_iota(jnp.int32, (L, L), 1)
    one_row = jnp.where(lax.broadcasted_iota(jnp.int32, (PACKED_SUBLANES, L), 0) == 0, 1.0, 0.0).astype(BF16)
    bcast = lambda col: jnp.broadcast_to(col, (N_HD, L))

    def operands(u, p):
        d, hh = divmod(p, MLSTM_HEADS)
        hs = slice(hh * hd, (hh + 1) * hd)
        if d == 0:
            tl = slice(u * L, (u + 1) * L)
            return qtf_ref[hs, tl], kf_ref[tl, hs], vtf_ref[hs, tl], htf_ref, hs, tl
        tl = slice((n_sub - 1 - u) * L, (n_sub - u) * L)
        return qtb_ref[hs, tl], kb_ref[tl, hs], vtb_ref[hs, tl], htb_ref, hs, tl

    gates = []
    for u in range(n_sub):
        fl = slice(u * L, (u + 1) * L)
        bl = slice((n_sub - 1 - u) * L, (n_sub - u) * L)
        ig = jnp.where(row_fwd, gtf_ref[:N_HD, fl], gtb_ref[:N_HD, bl]) + bias_ref[:N_HD]
        fp = jnp.where(row_fwd, gtf_ref[N_HD:, fl], gtb_ref[N_HD:, bl]) + bias_ref[N_HD:]
        lf = jax.nn.log_sigmoid(fp)
        lf2 = jnp.concatenate([jnp.where(row_fwd, lf, 0.0), jnp.where(row_fwd, 0.0, lf)], axis=1)
        bc = _split_dot(lf2, tri_ref[...])
        gs = ig - bc
        btot = jnp.where(row_fwd, bcast(bc[:, L - 1:L]), bcast(bc[:, 0:1]))
        gmax = bcast(jnp.max(gs, axis=1, keepdims=True))
        gates.append((bc, gs, btot, gmax, jnp.exp(gs - gmax)))

    scores, local = {}, {}
    for u in range(n_sub):
        w_st = gates[u][4]
        for p in range(N_HD):
            qt, k, vt, _, _, _ = operands(u, p)
            scores[u, p] = _dot(k, qt)
            vaug = jnp.concatenate([vt, one_row], axis=0)
            vw = (vaug.astype(F32) * w_st[p:p + 1, :]).astype(BF16)
            local[u, p] = _dot(vw, k)

    for u in range(n_sub):
        bc, gs, btot, gmax, _ = gates[u]
        m_prev = m_st[...]
        m_loc = btot + gmax
        m_new = jnp.maximum(btot + m_prev, m_loc)
        dec = jnp.exp(btot + m_prev - m_new)
        inj = jnp.exp(m_loc - m_new)
        m_st[...] = m_new
        for p in range(N_HD):
            qt, k, vt, dst, hs, tl = operands(u, p)
            seen = (src_row <= out_col) if p < MLSTM_HEADS else (src_row >= out_col)
            gs_seen = jnp.where(seen, jnp.broadcast_to(gs[p:p + 1, :], (L, L)).T, -jnp.inf)
            xm = jnp.maximum(m_prev[p:p + 1, :], jnp.max(gs_seen, axis=0, keepdims=True))
            pt = (scores[u, p] * jnp.exp(gs_seen - xm)).astype(BF16)
            qw = (qt.astype(F32) * jnp.exp(m_prev[p:p + 1, :] - xm)).astype(BF16)
            vaug = jnp.concatenate([vt, one_row], axis=0)
            c_prev = c_st[p]
            lhs = jnp.concatenate([vaug, c_prev.astype(BF16)], axis=1)
            rhs = jnp.concatenate([pt, qw], axis=0)
            o = _dot(lhs, rhs)
            den = jnp.maximum(jnp.abs(o[hd:hd + 1]), jnp.exp(-(bc[p:p + 1, :] + xm)))
            dst[hs, tl] = o[:hd] * (1.0 / den)
            c_st[p] = dec[p:p + 1, :] * c_prev + inj[p:p + 1, :] * local[u, p]

    @pl.when(j == pl.num_programs(1) - 1)
    def _():
        cfin_ref[...] = c_st[...]
        mfin_ref[...] = m_st[...]


def _mlstm(qt, k, vt, gt, bias, c0, m0):
    b, c, s = qt.shape
    tb = min(SCAN_CHUNKS * CHUNK, s)
    nb = s // tb
    tri = _tri_ones()
    fwd_t = lambda bi, j: (bi, 0, j)
    bwd_t = lambda bi, j: (bi, 0, nb - 1 - j)
    fwd_n = lambda bi, j: (bi, j, 0)
    bwd_n = lambda bi, j: (bi, nb - 1 - j, 0)
    st_c = pl.BlockSpec((None, N_HD, STATE_ROWS, MLSTM_HD), lambda bi, j: (bi, 0, 0, 0))
    st_m = pl.BlockSpec((None, N_HD, CHUNK), lambda bi, j: (bi, 0, 0))
    side = lambda ti, ni: [pl.BlockSpec((None, c, tb), ti), pl.BlockSpec((None, tb, c), ni),
                           pl.BlockSpec((None, c, tb), ti), pl.BlockSpec((None, GATE_ROWS, tb), ti)]
    return pl.pallas_call(
        _mlstm_kernel,
        grid=(b, nb),
        in_specs=side(fwd_t, fwd_n) + side(bwd_t, bwd_n) + [_full(bias.shape), _full(tri.shape), st_c, st_m],
        out_specs=[pl.BlockSpec((None, c, tb), fwd_t), pl.BlockSpec((None, c, tb), bwd_t), st_c, st_m],
        out_shape=[jax.ShapeDtypeStruct((b, c, s), F32),
                   jax.ShapeDtypeStruct((b, c, s), F32),
                   jax.ShapeDtypeStruct(c0.shape, F32),
                   jax.ShapeDtypeStruct(m0.shape, F32)],
        scratch_shapes=[pltpu.VMEM((N_HD, STATE_ROWS, MLSTM_HD), F32),
                        pltpu.VMEM((N_HD, CHUNK), F32)],
        compiler_params=_cparams(2, 56),
        name="mlstm",
    )(qt, k, vt, gt, qt, k, vt, gt, bias, tri, c0, m0)


DFT_GROUP = SUBLANES


def _dft_tables(s, n2):
    n1 = s // n2
    k1 = np.arange(n1, dtype=np.float64)
    s1 = np.arange(n1, dtype=np.float64)
    ma = np.zeros((n2, 2 * n1, 2 * n1), np.float64)
    for s2 in range(n2):
        th = 2.0 * np.pi * np.outer(k1, s1 * n2 + s2) / s
        c, sn = np.cos(th), np.sin(th)
        ma[s2] = np.block([[c, sn], [-sn, c]])
    k2 = np.arange(n2, dtype=np.float64)
    ph = 2.0 * np.pi * np.outer(k2, k2) / n2
    eye = np.eye(DFT_GROUP)
    scale = 1.0 / math.sqrt(s)
    lb = np.einsum("ksr,ab->kasrb", np.stack([np.cos(ph), np.sin(ph)], axis=-1) * scale, eye)
    lb = lb.reshape(n2 * DFT_GROUP, n2 * 2 * DFT_GROUP)
    return jnp.asarray(ma, F32).astype(BF16), jnp.asarray(lb, F32).astype(BF16)


def _seq_dft_kernel(n_a, z_ref, ma_ref, lb_ref, y_ref, t_scr):
    j = pl.program_id(1)
    g = DFT_GROUP
    n2 = t_scr.shape[0]
    n1 = t_scr.shape[2]

    @pl.when(j < n_a)
    def _():
        for s in range(g):
            zs = z_ref[:, s, :]
            rhs = jnp.concatenate([zs[:, :D_FOURIER], zs[:, D_FOURIER:]], axis=0).astype(BF16)
            t_scr[j * g + s] = _dot(ma_ref[s], rhs).reshape(2, n1, D_FOURIER)

    @pl.when(j >= n_a)
    def _():
        for q in range(y_ref.shape[1] // g):
            k0 = pl.multiple_of((j - n_a) * y_ref.shape[1] + q * g, g)
            rhs = t_scr[:, :, pl.ds(k0, g), :].reshape(n2 * 2 * g, D_FOURIER).astype(BF16)
            y_ref[:, q * g:(q + 1) * g, :] = _dot(lb_ref[...], rhs).reshape(n2, g, D_FOURIER)


def _seq_dft(z, n2=64):
    b, s, w = z.shape
    n1 = s // n2
    g = DFT_GROUP
    n_a = n2 // g
    gb = 2 * g
    n_b = n1 // gb
    ma, lb = _dft_tables(s, n2)
    y = pl.pallas_call(
        functools.partial(_seq_dft_kernel, n_a),
        grid=(b, n_a + n_b),
        in_specs=[pl.BlockSpec((None, n1, g, w), lambda bi, j: (bi, 0, jnp.minimum(j, n_a - 1), 0)),
                  pl.BlockSpec((g, 2 * n1, 2 * n1), lambda bi, j: (jnp.minimum(j, n_a - 1), 0, 0)),
                  _full(lb.shape)],
        out_specs=pl.BlockSpec((None, n2, gb, D_FOURIER), lambda bi, j: (bi, 0, jnp.maximum(j - n_a, 0), 0)),
        out_shape=jax.ShapeDtypeStruct((b, n2, n1, D_FOURIER), F32),
        scratch_shapes=[pltpu.VMEM((n2, 2, n1, D_FOURIER), F32)],
        compiler_params=_cparams(2, 60),
        name="seq_dft",
    )(z.reshape(b, n1, n2, w), ma, lb)
    return y.reshape(b, s, D_FOURIER)


def _dense_dft_kernel(z_ref, ld_ref, y_ref):
    rhs = jnp.concatenate([z_ref[:, :D_FOURIER], z_ref[:, D_FOURIER:]], axis=0).astype(BF16)
    y_ref[...] = _dot(ld_ref[...], rhs)


def _dense_dft(z):
    b, s, w = z.shape
    k = np.arange(s, dtype=np.float64)
    th = 2.0 * np.pi * np.outer(k, k) / s
    ld = jnp.asarray(np.concatenate([np.cos(th), np.sin(th)], axis=1) / math.sqrt(s), F32).astype(BF16)
    return pl.pallas_call(
        _dense_dft_kernel,
        grid=(b,),
        in_specs=[pl.BlockSpec((None, s, w), lambda bi: (bi, 0, 0)), _full(ld.shape)],
        out_specs=pl.BlockSpec((None, s, D_FOURIER), lambda bi: (bi, 0, 0)),
        out_shape=jax.ShapeDtypeStruct((b, s, D_FOURIER), F32),
        compiler_params=_cparams(1, 32),
        name="dft_dense",
    )(z, ld)


FFN_LEAD_CHUNKS = 8


def _mixer_out(x_ref, yf_ref, htf_ref, htb_ref, zt_ref, mod, nw_ref, hnw_ref, wom_ref):
    _, _, gate = _mod3(mod, 1)
    hd = MLSTM_HD
    hs = htf_ref[...] + htb_ref[...]
    parts = []
    for h in range(MLSTM_HEADS):
        seg = hs[h * hd:(h + 1) * hd]
        mu = jnp.mean(seg, axis=0, keepdims=True)
        cen = seg - mu
        var = jnp.mean(cen * cen, axis=0, keepdims=True)
        parts.append(cen * lax.rsqrt(var + EPS))
    hn = jnp.concatenate(parts, axis=0) * hnw_ref[...]
    ymt = (hn * jax.nn.sigmoid(zt_ref[...])).astype(BF16)
    y = (_dot(yf_ref[...].astype(BF16), wom_ref[:D_FOURIER])
         + lax.dot_general(ymt, wom_ref[D_FOURIER:], TN, preferred_element_type=F32))
    return x_ref[...] + gate * _rms(y, nw_ref[3:4])


def _combine_ffn_kernel(x_ref, yf_ref, htf_ref, htb_ref, zt_ref, modc_ref, modf_ref, nw_ref, hnw_ref,
                        wom_ref, wi_ref, wo_ref, o_ref, a_ref, x2_a, h_a, x2_b, h_b):
    j = pl.program_id(0)

    @pl.when(j == 0)
    def _():
        x2_b[...] = jnp.zeros_like(x2_b)
        h_b[...] = jnp.zeros_like(h_b)

    def step(x2_rd, h_rd, x2_wr, h_wr):
        h = h_rd[...]
        _ffn_chunks(h, wi_ref, a_ref, 0, FFN_LEAD_CHUNKS)
        modc = modc_ref[...]
        x2 = _mixer_out(x_ref, yf_ref, htf_ref, htb_ref, zt_ref, modc, nw_ref, hnw_ref, wom_ref)
        x2_wr[...] = x2
        h_wr[...] = _ffn_prenorm(x2, modc, 2, nw_ref)
        _ffn_chunks(h, wi_ref, a_ref, FFN_LEAD_CHUNKS, N_FF_CHUNKS)
        o_ref[...] = _ffn_finish(x2_rd[...], modf_ref[...], 2, nw_ref, wo_ref, a_ref)

    @pl.when(jnp.bitwise_and(j, 1) == 0)
    def _():
        step(x2_b, h_b, x2_a, h_a)

    @pl.when(jnp.bitwise_and(j, 1) == 1)
    def _():
        step(x2_a, h_a, x2_b, h_b)


def _combine_ffn(x, yf, htf, htb, zt, mod, l, norm_w, hnw, w_mix, ff_in, ff_out):
    b, s, d = x.shape
    tm = min(512, s)
    nt = s // tm
    n_tiles = b * nt
    mixed = lambda j: jnp.minimum(j, n_tiles - 1)
    ffn = lambda j: jnp.maximum(j - 1, 0)
    tok = lambda w, t: pl.BlockSpec((None, tm, w), lambda j: (t(j) // nt, t(j) % nt, 0))
    tr = pl.BlockSpec((None, D_MLSTM, tm), lambda j: (mixed(j) // nt, 0, mixed(j) % nt))
    shared = mod.shape[0] == 1
    mod_of = lambda t: pl.BlockSpec((None, N_MOD, d), lambda j: (0 if shared else t(j) // nt, 0, 0))
    hnw_b = jnp.broadcast_to(hnw.reshape(D_MLSTM, 1), (D_MLSTM, tm))
    return pl.pallas_call(
        _combine_ffn_kernel,
        grid=(n_tiles + 1,),
        in_specs=[tok(d, mixed), tok(D_FOURIER, mixed), tr, tr, tr, mod_of(mixed), mod_of(ffn),
                  _resident(norm_w.shape, (l,)), _resident(hnw_b.shape), _resident(w_mix.shape, (l,)),
                  _resident(ff_in.shape, (l, 1)), _resident(ff_out.shape, (l, 1))],
        out_specs=tok(d, ffn),
        out_shape=jax.ShapeDtypeStruct(x.shape, F32),
        scratch_shapes=[pltpu.VMEM((tm, D_FF), BF16),
                        pltpu.VMEM((tm, d), F32), pltpu.VMEM((tm, d), BF16),
                        pltpu.VMEM((tm, d), F32), pltpu.VMEM((tm, d), BF16)],
        compiler_params=_cparams(1, 60),
        name="combine_ffn",
    )(x, yf, htf, htb, zt, mod, mod, norm_w, hnw_b, w_mix, ff_in, ff_out)


def _gate_layout(w_gate, gate_b):
    h = MLSTM_HEADS
    wt = w_gate.T
    w = jnp.concatenate([wt[0:h], wt[2 * h:3 * h], wt[h:2 * h], wt[3 * h:4 * h]], axis=0)
    bias = jnp.concatenate([gate_b[0], gate_b[2], gate_b[1], gate_b[3]])
    return w, jnp.broadcast_to(bias.reshape(GATE_ROWS, 1), (GATE_ROWS, CHUNK))


def kernel(x, c, ctx, c_ctx, w_ada, b_ada, norm_w, w_ff_in, w_ff_out, w_in, w_fmix,
           conv_w, conv_b, w_qkv, gate_b, mlstm_norm_w, w_out):
    b = x.shape[0]
    d = D_MODEL
    n_cond = b + 1
    assert n_cond <= SUBLANES
    cc = jnp.concatenate([c, c_ctx[None], jnp.zeros((SUBLANES - n_cond, d), F32)], axis=0)
    mod = _modulation(cc, w_ada, b_ada).reshape(DEPTH, SUBLANES, N_MOD, d)
    g_mix = _fold_fourier(w_fmix)
    w_fm = w_in[:, :, :D_FOURIER + D_MLSTM].astype(BF16)
    ff_in = w_ff_in.astype(BF16)
    ff_out = w_ff_out.astype(BF16)
    w_mix = w_out.astype(BF16)

    xl, xc = x, ctx
    c_zero = jnp.zeros((b, N_HD, STATE_ROWS, MLSTM_HD), F32)
    m_zero = jnp.zeros((b, N_HD, CHUNK), F32)
    for l in range(DEPTH):
        last = l == DEPTH - 1
        mod_l = mod[l, :b]
        mod_c = mod[l, b:b + 1]
        o_z = D_FOURIER + D_MLSTM
        o_g = D_FOURIER + 2 * D_MLSTM
        w_gate, g_bias = _gate_layout(w_in[l, :, o_g:], gate_b[l])
        pw = (w_fm, g_mix, jnp.concatenate([w_in[l, :, o_z:o_g].T, w_gate], axis=0).astype(BF16))
        cw = conv_w[l].reshape(9, D_MLSTM)
        cb = conv_b[l].reshape(1, D_MLSTM)
        qkvw = (jnp.swapaxes(w_qkv[l, 0], 1, 2).astype(BF16), w_qkv[l, 1].astype(BF16),
                jnp.swapaxes(w_qkv[l, 2], 1, 2).astype(BF16))
        hnw = mlstm_norm_w[l]

        xl, z_l, xm_l, zt_l, gt_l = _ffn_inproj(xl, mod_l, l, norm_w, ff_in, ff_out, *pw)
        xc, z_c, xm_c, zt_c, gt_c = _ffn_inproj(xc, mod_c, l, norm_w, ff_in, ff_out, *pw)
        qkv_l = _convqkv(xm_l, GRID_W, cw, cb, *qkvw)
        qkv_c = _convqkv(xm_c, xm_c.shape[1], cw, cb, *qkvw)
        htf_c, htb_c, c_ctx_st, m_ctx_st = _mlstm(*qkv_c, gt_c, g_bias, c_zero, m_zero)
        htf_l, htb_l, _, _ = _mlstm(*qkv_l, gt_l, g_bias, c_ctx_st, m_ctx_st)
        yf_l = _seq_dft(z_l)
        xl = _combine_ffn(xl, yf_l, htf_l, htb_l, zt_l, mod_l, l, norm_w, hnw, w_mix, ff_in, ff_out)
        if not last:
            yf_c = _dense_dft(z_c)
            xc = _combine_ffn(xc, yf_c, htf_c, htb_c, zt_c, mod_c, l, norm_w, hnw, w_mix, ff_in, ff_out)
    return xl
```

```python
import functools
import math

import numpy as np
import jax
import jax.numpy as jnp
from jax import lax
from jax.experimental import pallas as pl
from jax.experimental.pallas import tpu as pltpu

D_MODEL = 1024
DEPTH = 2
GRID_W = 64
D_FOURIER = 512
FOURIER_GROUPS = 8
FOURIER_GW = 64
D_MLSTM = 512
MLSTM_HEADS = 4
MLSTM_HD = 128
CHUNK = 128
D_FF = 2816
N_MOD = 9
FFN_RES = 0.5
EPS = 1e-6

LANES = 128
SUBLANES = 8
PACKED_SUBLANES = 16
MXU_DIM = 256
VMEM_BYTES = 64 * 1024 * 1024
N_HD = 2 * MLSTM_HEADS
STATE_ROWS = MLSTM_HD + PACKED_SUBLANES
GATE_ROWS = 2 * N_HD
SCAN_CHUNKS = 8

F32 = jnp.float32
BF16 = jnp.bfloat16
NT = (((1,), (1,)), ((), ()))
TN = (((0,), (0,)), ((), ()))


def _cparams(n_grid, vmem_mb):
    return pltpu.CompilerParams(
        dimension_semantics=("arbitrary",) * n_grid,
        vmem_limit_bytes=min(vmem_mb * 1024 * 1024, VMEM_BYTES - 4 * 1024 * 1024))


def _dot(a, b):
    return jnp.dot(a, b, preferred_element_type=F32)


def _dot_nt(a, b):
    return lax.dot_general(a, b, NT, preferred_element_type=F32)


def _rms(x, w):
    return x * lax.rsqrt(jnp.mean(x * x, axis=-1, keepdims=True) + EPS) * w


def _mod3(m, j):
    return m[3 * j:3 * j + 1], m[3 * j + 1:3 * j + 2], m[3 * j + 2:3 * j + 3]


def _mod_index(mod):
    if mod.shape[0] == 1:
        return lambda b, i: (0, 0, 0)
    return lambda b, i: (b, 0, 0)


def _full(shape):
    return pl.BlockSpec(shape, lambda *_: (0,) * len(shape))


def _resident(shape, lead=()):
    rest = tuple(shape[len(lead):])
    index = tuple(lead) + (0,) * len(rest)
    return pl.BlockSpec((None,) * len(lead) + rest, lambda *_: index, pipeline_mode=pl.Buffered(1))


def _mod_kernel(c_ref, w_ref, b_ref, o_ref):
    c = c_ref[...]
    s = (c * jax.nn.sigmoid(c)).astype(BF16)
    o_ref[...] = _dot(s, w_ref[...].astype(BF16)) + b_ref[...]


def _modulation(cc, w_ada, b_ada):
    d = D_MODEL
    return pl.pallas_call(
        _mod_kernel,
        grid=(DEPTH, N_MOD),
        in_specs=[pl.BlockSpec((SUBLANES, d), lambda l, n: (0, 0)),
                  pl.BlockSpec((None, d, d), lambda l, n: (l, 0, n)),
                  pl.BlockSpec((None, 1, d), lambda l, n: (l, 0, n))],
        out_specs=pl.BlockSpec((None, SUBLANES, d), lambda l, n: (l, 0, n)),
        out_shape=jax.ShapeDtypeStruct((DEPTH, SUBLANES, N_MOD * d), F32),
        compiler_params=_cparams(2, 32),
        name="modulation",
    )(cc, w_ada, b_ada.reshape(DEPTH, 1, N_MOD * d))


FF_CHUNK = MXU_DIM
N_FF_CHUNKS = D_FF // FF_CHUNK


def _ffn_prenorm(x, mod, j, nw_ref):
    shift, scale, _ = _mod3(mod, j)
    return (_rms(x, nw_ref[2 * j:2 * j + 1]) * (1.0 + scale) + shift).astype(BF16)


def _ffn_chunks(h, wi_ref, a_ref, lo, hi):
    for c in range(lo, hi):
        sl = slice(c * FF_CHUNK, (c + 1) * FF_CHUNK)
        g = _dot(h, wi_ref[:, sl])
        u = _dot(h, wi_ref[:, D_FF + c * FF_CHUNK:D_FF + (c + 1) * FF_CHUNK])
        a_ref[:, sl] = (g * jax.nn.sigmoid(g) * u).astype(BF16)


def _ffn_finish(x, mod, j, nw_ref, wo_ref, a_ref):
    _, _, gate = _mod3(mod, j)
    y = _dot(a_ref[...], wo_ref[...])
    return x + FFN_RES * gate * _rms(y, nw_ref[2 * j + 1:2 * j + 2])


def _ffn_body(x, mod, j, nw_ref, wi_ref, wo_ref, a_ref):
    _ffn_chunks(_ffn_prenorm(x, mod, j, nw_ref), wi_ref, a_ref, 0, N_FF_CHUNKS)
    return _ffn_finish(x, mod, j, nw_ref, wo_ref, a_ref)


def _fold_kernel(bdw_ref, bdc_ref, bds_ref, o_ref):
    hi = lax.Precision.HIGHEST
    bdw = bdw_ref[...]
    t = MXU_DIM
    for part, dft_ref in enumerate((bdc_ref, bds_ref)):
        g = jnp.dot(dft_ref[...], bdw, precision=hi, preferred_element_type=F32)
        for i in range(D_FOURIER // t):
            o_ref[part * (D_FOURIER // t) + i] = g[i * t:(i + 1) * t, i * t:(i + 1) * t].astype(BF16)


def _channel_dft_blocks():
    c = np.arange(FOURIER_GW, dtype=np.float64)
    ang = 2.0 * np.pi * np.outer(c, c) / FOURIER_GW
    eye = np.eye(FOURIER_GROUPS)
    scale = 1.0 / math.sqrt(FOURIER_GW)
    bdc = np.kron(eye, np.cos(ang) * scale)
    bds = np.kron(eye, -np.sin(ang) * scale)
    return jnp.asarray(bdc, F32), jnp.asarray(bds, F32)


N_MIX_TILES = 2 * D_FOURIER // MXU_DIM


def _fold_fourier(w_fmix):
    eye = jnp.eye(FOURIER_GROUPS, dtype=F32)
    bdw = jnp.einsum("gh,lgcd->lgchd", eye, w_fmix).reshape(DEPTH, D_FOURIER, D_FOURIER)
    bdc, bds = _channel_dft_blocks()
    sq = pl.BlockSpec((D_FOURIER, D_FOURIER), lambda l: (0, 0))
    return pl.pallas_call(
        _fold_kernel,
        grid=(DEPTH,),
        in_specs=[pl.BlockSpec((None, D_FOURIER, D_FOURIER), lambda l: (l, 0, 0)), sq, sq],
        out_specs=pl.BlockSpec((None, N_MIX_TILES, MXU_DIM, MXU_DIM), lambda l: (l, 0, 0, 0)),
        out_shape=jax.ShapeDtypeStruct((DEPTH, N_MIX_TILES, MXU_DIM, MXU_DIM), BF16),
        compiler_params=_cparams(1, 32),
        name="fold_fourier",
    )(bdw, bdc, bds)


def _ffn_inproj_kernel(x_ref, mod_ref, nw_ref, wi_ref, wo_ref, wfm_ref, g_ref, wt_ref,
                       o_ref, z_ref, xm_ref, zt_ref, gt_ref, a_ref):
    mod = mod_ref[...]
    x1 = _ffn_body(x_ref[...], mod, 0, nw_ref, wi_ref, wo_ref, a_ref)
    o_ref[...] = x1
    shift, scale, _ = _mod3(mod, 1)
    h = (_rms(x1, nw_ref[2:3]) * (1.0 + scale) + shift).astype(BF16)
    u = _dot(h, wfm_ref[...])
    xm_ref[...] = u[:, D_FOURIER:]
    xf = u[:, :D_FOURIER].astype(BF16)
    t = MXU_DIM
    n_src = D_FOURIER // t
    for i in range(N_MIX_TILES):
        src = xf[:, (i % n_src) * t:(i % n_src + 1) * t]
        z_ref[:, i * t:(i + 1) * t] = _dot(src, g_ref[i])
    tt = _dot_nt(wt_ref[...], h)
    zt_ref[...] = tt[:D_MLSTM]
    gt_ref[...] = tt[D_MLSTM:]


def _ffn_inproj(x, mod, l, norm_w, ff_in, ff_out, wfm, g_mix, wt):
    b, s, d = x.shape
    tm = min(1024, s)
    tok = lambda w: pl.BlockSpec((None, tm, w), lambda bi, i: (bi, i, 0))
    tr = lambda r: pl.BlockSpec((None, r, tm), lambda bi, i: (bi, 0, i))
    return pl.pallas_call(
        _ffn_inproj_kernel,
        grid=(b, s // tm),
        in_specs=[tok(d), pl.BlockSpec((None, N_MOD, d), _mod_index(mod)), _resident(norm_w.shape, (l,)),
                  _resident(ff_in.shape, (l, 0)), _resident(ff_out.shape, (l, 0)),
                  _resident(wfm.shape, (l,)), _resident(g_mix.shape, (l,)), _resident(wt.shape)],
        out_specs=[tok(d), tok(2 * D_FOURIER), tok(D_MLSTM), tr(D_MLSTM), tr(GATE_ROWS)],
        out_shape=[jax.ShapeDtypeStruct(x.shape, F32),
                   jax.ShapeDtypeStruct((b, s, 2 * D_FOURIER), F32),
                   jax.ShapeDtypeStruct((b, s, D_MLSTM), F32),
                   jax.ShapeDtypeStruct((b, D_MLSTM, s), F32),
                   jax.ShapeDtypeStruct((b, GATE_ROWS, s), F32)],
        scratch_shapes=[pltpu.VMEM((tm, D_FF), BF16)],
        compiler_params=_cparams(2, 60),
        name="ffn_inproj",
    )(x, mod, norm_w, ff_in, ff_out, wfm, g_mix, wt)


def _convqkv_kernel(width, rows, halo, prev_ref, cur_ref, next_ref, cw_ref, cb_ref,
                    wqt_ref, wk_ref, wvt_ref, qt_ref, k_ref, vt_ref):
    i = pl.program_id(1)
    tm = cur_ref.shape[0]
    cur = cur_ref[...]
    prev = jnp.where(i > 0, prev_ref[...], 0.0)
    nxt = jnp.where(i < pl.num_programs(1) - 1, next_ref[...], 0.0)
    ext = jnp.concatenate([prev, cur, nxt], axis=0)
    n = ext.shape[0]
    r = lax.broadcasted_iota(jnp.int32, ext.shape, 0)
    col = (r + (width - halo % width)) & (width - 1)
    left = jnp.where(col == 0, 0.0, pltpu.roll(ext, 1, axis=0))
    right = jnp.where(col == width - 1, 0.0, pltpu.roll(ext, n - 1, axis=0))
    taps = (left, ext, right)
    cw = cw_ref[...]
    acc = jnp.zeros((tm, cur.shape[1]), F32) + cb_ref[...]
    for dy in (-1, 0, 1):
        if rows == 1 and dy != 0:
            continue
        lo = halo + dy * width
        for dx in (-1, 0, 1):
            k = (dy + 1) * 3 + (dx + 1)
            acc = acc + cw[k:k + 1] * taps[dx + 1][lo:lo + tm]
    cv = (acc * jax.nn.sigmoid(acc)).astype(BF16)
    xm = cur.astype(BF16)
    hd = MLSTM_HD
    for h in range(MLSTM_HEADS):
        sl = slice(h * hd, (h + 1) * hd)
        qt_ref[sl, :] = _dot_nt(wqt_ref[h], cv[:, sl]).astype(BF16)
        k_ref[:, sl] = (_dot(cv[:, sl], wk_ref[h]) * hd ** -0.5).astype(BF16)
        vt_ref[sl, :] = _dot_nt(wvt_ref[h], xm[:, sl]).astype(BF16)


def _convqkv(xm, width, cw, cb, wqt, wk, wvt):
    b, s, c = xm.shape
    rows = s // width
    tm = min(512, s)
    halo = LANES if rows > 1 else SUBLANES
    assert tm % width == 0 and tm % halo == 0 and (rows == 1 or halo > width)
    assert width & (width - 1) == 0
    per = tm // halo
    nblk = s // halo
    kern = functools.partial(_convqkv_kernel, width, rows, halo)
    tr = pl.BlockSpec((None, c, tm), lambda bi, i: (bi, 0, i))
    return pl.pallas_call(
        kern,
        grid=(b, s // tm),
        in_specs=[pl.BlockSpec((None, halo, c), lambda bi, i: (bi, jnp.maximum(i * per - 1, 0), 0)),
                  pl.BlockSpec((None, tm, c), lambda bi, i: (bi, i, 0)),
                  pl.BlockSpec((None, halo, c), lambda bi, i: (bi, jnp.minimum((i + 1) * per, nblk - 1), 0)),
                  _full(cw.shape), _full(cb.shape), _full(wqt.shape), _full(wk.shape), _full(wvt.shape)],
        out_specs=[tr, pl.BlockSpec((None, tm, c), lambda bi, i: (bi, i, 0)), tr],
        out_shape=[jax.ShapeDtypeStruct((b, c, s), BF16),
                   jax.ShapeDtypeStruct((b, s, c), BF16),
                   jax.ShapeDtypeStruct((b, c, s), BF16)],
        compiler_params=_cparams(2, 32),
        name="convqkv",
    )(xm, xm, xm, cw, cb, wqt, wk, wvt)


def _tri_ones():
    u = np.arange(CHUNK)[:, None]
    t = np.arange(CHUNK)[None, :]
    return jnp.asarray(np.concatenate([u <= t, u >= t], axis=0), BF16)


def _split_dot(x, w):
    hi = x.astype(BF16)
    r1 = x - hi.astype(F32)
    mid = r1.astype(BF16)
    lo = (r1 - mid.astype(F32)).astype(BF16)
    return _dot(hi, w) + _dot(mid, w) + _dot(lo, w)


def _mlstm_kernel(qtf_ref, kf_ref, vtf_ref, gtf_ref, qtb_ref, kb_ref, vtb_ref, gtb_ref,
                  bias_ref, tri_ref, c0_ref, m0_ref, htf_ref, htb_ref, cfin_ref, mfin_ref, c_st, m_st):
    j = pl.program_id(1)
    L = CHUNK
    hd = MLSTM_HD
    n_sub = gtf_ref.shape[1] // L

    @pl.when(j == 0)
    def _():
        c_st[...] = c0_ref[...]
        m_st[...] = m0_ref[...]

    row_fwd = lax.broadcasted_iota(jnp.int32, (N_HD, L), 0) < MLSTM_HEADS
    src_row = lax.broadcasted_iota(jnp.int32, (L, L), 0)
    out_col = lax.broadcasted_iota(jnp.int32, (L, L), 1)
    one_row = jnp.where(lax.broadcasted_iota(jnp.int32, (PACKED_SUBLANES, L), 0) == 0, 1.0, 0.0).astype(BF16)
    bcast = lambda col: jnp.broadcast_to(col, (N_HD, L))

    def operands(u, p):
        d, hh = divmod(p, MLSTM_HEADS)
        hs = slice(hh * hd, (hh + 1) * hd)
        if d == 0:
            tl = slice(u * L, (u + 1) * L)
            return qtf_ref[hs, tl], kf_ref[tl, hs], vtf_ref[hs, tl], htf_ref, hs, tl
        tl = slice((n_sub - 1 - u) * L, (n_sub - u) * L)
        return qtb_ref[hs, tl], kb_ref[tl, hs], vtb_ref[hs, tl], htb_ref, hs, tl

    gates = []
    for u in range(n_sub):
        fl = slice(u * L, (u + 1) * L)
        bl = slice((n_sub - 1 - u) * L, (n_sub - u) * L)
        ig = jnp.where(row_fwd, gtf_ref[:N_HD, fl], gtb_ref[:N_HD, bl]) + bias_ref[:N_HD]
        fp = jnp.where(row_fwd, gtf_ref[N_HD:, fl], gtb_ref[N_HD:, bl]) + bias_ref[N_HD:]
        lf = jax.nn.log_sigmoid(fp)
        lf2 = jnp.concatenate([jnp.where(row_fwd, lf, 0.0), jnp.where(row_fwd, 0.0, lf)], axis=1)
        bc = _split_dot(lf2, tri_ref[...])
        gs = ig - bc
        btot = jnp.where(row_fwd, bcast(bc[:, L - 1:L]), bcast(bc[:, 0:1]))
        gmax = bcast(jnp.max(gs, axis=1, keepdims=True))
        gates.append((bc, gs, btot, gmax, jnp.exp(gs - gmax)))

    scores, local = {}, {}
    for u in range(n_sub):
        w_st = gates[u][4]
        for p in range(N_HD):
            qt, k, vt, _, _, _ = operands(u, p)
            scores[u, p] = _dot(k, qt)
            vaug = jnp.concatenate([vt, one_row], axis=0)
            vw = (vaug.astype(F32) * w_st[p:p + 1, :]).astype(BF16)
            local[u, p] = _dot(vw, k)

    for u in range(n_sub):
        bc, gs, btot, gmax, _ = gates[u]
        m_prev = m_st[...]
        m_loc = btot + gmax
        m_new = jnp.maximum(btot + m_prev, m_loc)
        dec = jnp.exp(btot + m_prev - m_new)
        inj = jnp.exp(m_loc - m_new)
        m_st[...] = m_new
        for p in range(N_HD):
            qt, k, vt, dst, hs, tl = operands(u, p)
            seen = (src_row <= out_col) if p < MLSTM_HEADS else (src_row >= out_col)
            gs_seen = jnp.where(seen, jnp.broadcast_to(gs[p:p + 1, :], (L, L)).T, -jnp.inf)
            xm = jnp.maximum(m_prev[p:p + 1, :], jnp.max(gs_seen, axis=0, keepdims=True))
            pt = (scores[u, p] * jnp.exp(gs_seen - xm)).astype(BF16)
            qw = (qt.astype(F32) * jnp.exp(m_prev[p:p + 1, :] - xm)).astype(BF16)
            vaug = jnp.concatenate([vt, one_row], axis=0)
            c_prev = c_st[p]
            lhs = jnp.concatenate([vaug, c_prev.astype(BF16)], axis=1)
            rhs = jnp.concatenate([pt, qw], axis=0)
            o = _dot(lhs, rhs)
            den = jnp.maximum(jnp.abs(o[hd:hd + 1]), jnp.exp(-(bc[p:p + 1, :] + xm)))
            dst[hs, tl] = o[:hd] * (1.0 / den)
            c_st[p] = dec[p:p + 1, :] * c_prev + inj[p:p + 1, :] * local[u, p]

    @pl.when(j == pl.num_programs(1) - 1)
    def _():
        cfin_ref[...] = c_st[...]
        mfin_ref[...] = m_st[...]


def _mlstm(qt, k, vt, gt, bias, c0, m0):
    b, c, s = qt.shape
    tb = min(SCAN_CHUNKS * CHUNK, s)
    nb = s // tb
    tri = _tri_ones()
    fwd_t = lambda bi, j: (bi, 0, j)
    bwd_t = lambda bi, j: (bi, 0, nb - 1 - j)
    fwd_n = lambda bi, j: (bi, j, 0)
    bwd_n = lambda bi, j: (bi, nb - 1 - j, 0)
    st_c = pl.BlockSpec((None, N_HD, STATE_ROWS, MLSTM_HD), lambda bi, j: (bi, 0, 0, 0))
    st_m = pl.BlockSpec((None, N_HD, CHUNK), lambda bi, j: (bi, 0, 0))
    side = lambda ti, ni: [pl.BlockSpec((None, c, tb), ti), pl.BlockSpec((None, tb, c), ni),
                           pl.BlockSpec((None, c, tb), ti), pl.BlockSpec((None, GATE_ROWS, tb), ti)]
    return pl.pallas_call(
        _mlstm_kernel,
        grid=(b, nb),
        in_specs=side(fwd_t, fwd_n) + side(bwd_t, bwd_n) + [_full(bias.shape), _full(tri.shape), st_c, st_m],
        out_specs=[pl.BlockSpec((None, c, tb), fwd_t), pl.BlockSpec((None, c, tb), bwd_t), st_c, st_m],
        out_shape=[jax.ShapeDtypeStruct((b, c, s), F32),
                   jax.ShapeDtypeStruct((b, c, s), F32),
                   jax.ShapeDtypeStruct(c0.shape, F32),
                   jax.ShapeDtypeStruct(m0.shape, F32)],
        scratch_shapes=[pltpu.VMEM((N_HD, STATE_ROWS, MLSTM_HD), F32),
                        pltpu.VMEM((N_HD, CHUNK), F32)],
        compiler_params=_cparams(2, 56),
        name="mlstm",
    )(qt, k, vt, gt, qt, k, vt, gt, bias, tri, c0, m0)


DFT_GROUP = SUBLANES


def _dft_tables(s, n2):
    n1 = s // n2
    k1 = np.arange(n1, dtype=np.float64)
    s1 = np.arange(n1, dtype=np.float64)
    ma = np.zeros((n2, 2 * n1, 2 * n1), np.float64)
    for s2 in range(n2):
        th = 2.0 * np.pi * np.outer(k1, s1 * n2 + s2) / s
        c, sn = np.cos(th), np.sin(th)
        ma[s2] = np.block([[c, sn], [-sn, c]])
    k2 = np.arange(n2, dtype=np.float64)
    ph = 2.0 * np.pi * np.outer(k2, k2) / n2
    eye = np.eye(DFT_GROUP)
    scale = 1.0 / math.sqrt(s)
    lb = np.einsum("ksr,ab->kasrb", np.stack([np.cos(ph), np.sin(ph)], axis=-1) * scale, eye)
    lb = lb.reshape(n2 * DFT_GROUP, n2 * 2 * DFT_GROUP)
    return jnp.asarray(ma, F32).astype(BF16), jnp.asarray(lb, F32).astype(BF16)


def _seq_dft_kernel(n_a, z_ref, ma_ref, lb_ref, y_ref, t_scr):
    j = pl.program_id(1)
    g = DFT_GROUP
    n2 = t_scr.shape[0]
    n1 = t_scr.shape[2]

    @pl.when(j < n_a)
    def _():
        zt = pltpu.einshape("abc->bac", z_ref[...])
        for s in range(g):
            zs = zt[s]
            rhs = jnp.concatenate([zs[:, :D_FOURIER], zs[:, D_FOURIER:]], axis=0).astype(BF16)
            t_scr[j * g + s] = _dot(ma_ref[s], rhs).reshape(2, n1, D_FOURIER)

    @pl.when(j >= n_a)
    def _():
        for q in range(y_ref.shape[1] // g):
            k0 = pl.multiple_of((j - n_a) * y_ref.shape[1] + q * g, g)
            rhs = t_scr[:, :, pl.ds(k0, g), :].reshape(n2 * 2 * g, D_FOURIER).astype(BF16)
            y_ref[:, q * g:(q + 1) * g, :] = _dot(lb_ref[...], rhs).reshape(n2, g, D_FOURIER)


def _seq_dft(z, n2=64):
    b, s, w = z.shape
    n1 = s // n2
    g = DFT_GROUP
    n_a = n2 // g
    gb = 2 * g
    n_b = n1 // gb
    ma, lb = _dft_tables(s, n2)
    y = pl.pallas_call(
        functools.partial(_seq_dft_kernel, n_a),
        grid=(b, n_a + n_b),
        in_specs=[pl.BlockSpec((None, n1, g, w), lambda bi, j: (bi, 0, jnp.minimum(j, n_a - 1), 0)),
                  pl.BlockSpec((g, 2 * n1, 2 * n1), lambda bi, j: (jnp.minimum(j, n_a - 1), 0, 0)),
                  _full(lb.shape)],
        out_specs=pl.BlockSpec((None, n2, gb, D_FOURIER), lambda bi, j: (bi, 0, jnp.maximum(j - n_a, 0), 0)),
        out_shape=jax.ShapeDtypeStruct((b, n2, n1, D_FOURIER), F32),
        scratch_shapes=[pltpu.VMEM((n2, 2, n1, D_FOURIER), F32)],
        compiler_params=_cparams(2, 60),
        name="seq_dft",
    )(z.reshape(b, n1, n2, w), ma, lb)
    return y.reshape(b, s, D_FOURIER)


def _dense_dft_kernel(z_ref, ld_ref, y_ref):
    rhs = jnp.concatenate([z_ref[:, :D_FOURIER], z_ref[:, D_FOURIER:]], axis=0).astype(BF16)
    y_ref[...] = _dot(ld_ref[...], rhs)


def _dense_dft(z):
    b, s, w = z.shape
    k = np.arange(s, dtype=np.float64)
    th = 2.0 * np.pi * np.outer(k, k) / s
    ld = jnp.asarray(np.concatenate([np.cos(th), np.sin(th)], axis=1) / math.sqrt(s), F32).astype(BF16)
    return pl.pallas_call(
        _dense_dft_kernel,
        grid=(b,),
        in_specs=[pl.BlockSpec((None, s, w), lambda bi: (bi, 0, 0)), _full(ld.shape)],
        out_specs=pl.BlockSpec((None, s, D_FOURIER), lambda bi: (bi, 0, 0)),
        out_shape=jax.ShapeDtypeStruct((b, s, D_FOURIER), F32),
        compiler_params=_cparams(1, 32),
        name="dft_dense",
    )(z, ld)


FFN_LEAD_CHUNKS = 8


def _mixer_out(x_ref, yf_ref, htf_ref, htb_ref, zt_ref, mod, nw_ref, hnw_ref, wom_ref):
    _, _, gate = _mod3(mod, 1)
    hd = MLSTM_HD
    hs = htf_ref[...] + htb_ref[...]
    parts = []
    for h in range(MLSTM_HEADS):
        seg = hs[h * hd:(h + 1) * hd]
        mu = jnp.mean(seg, axis=0, keepdims=True)
        cen = seg - mu
        var = jnp.mean(cen * cen, axis=0, keepdims=True)
        parts.append(cen * lax.rsqrt(var + EPS))
    hn = jnp.concatenate(parts, axis=0) * hnw_ref[...]
    ymt = (hn * jax.nn.sigmoid(zt_ref[...])).astype(BF16)
    y = (_dot(yf_ref[...].astype(BF16), wom_ref[:D_FOURIER])
         + lax.dot_general(ymt, wom_ref[D_FOURIER:], TN, preferred_element_type=F32))
    return x_ref[...] + gate * _rms(y, nw_ref[3:4])


def _combine_ffn_kernel(x_ref, yf_ref, htf_ref, htb_ref, zt_ref, modc_ref, modf_ref, nw_ref, hnw_ref,
                        wom_ref, wi_ref, wo_ref, o_ref, a_ref, x2_a, h_a, x2_b, h_b):
    j = pl.program_id(0)

    @pl.when(j == 0)
    def _():
        x2_b[...] = jnp.zeros_like(x2_b)
        h_b[...] = jnp.zeros_like(h_b)

    def step(x2_rd, h_rd, x2_wr, h_wr):
        h = h_rd[...]
        _ffn_chunks(h, wi_ref, a_ref, 0, FFN_LEAD_CHUNKS)
        modc = modc_ref[...]
        x2 = _mixer_out(x_ref, yf_ref, htf_ref, htb_ref, zt_ref, modc, nw_ref, hnw_ref, wom_ref)
        x2_wr[...] = x2
        h_wr[...] = _ffn_prenorm(x2, modc, 2, nw_ref)
        _ffn_chunks(h, wi_ref, a_ref, FFN_LEAD_CHUNKS, N_FF_CHUNKS)
        o_ref[...] = _ffn_finish(x2_rd[...], modf_ref[...], 2, nw_ref, wo_ref, a_ref)

    @pl.when(jnp.bitwise_and(j, 1) == 0)
    def _():
        step(x2_b, h_b, x2_a, h_a)

    @pl.when(jnp.bitwise_and(j, 1) == 1)
    def _():
        step(x2_a, h_a, x2_b, h_b)


def _combine_ffn(x, yf, htf, htb, zt, mod, l, norm_w, hnw, w_mix, ff_in, ff_out):
    b, s, d = x.shape
    tm = min(512, s)
    nt = s // tm
    n_tiles = b * nt
    mixed = lambda j: jnp.minimum(j, n_tiles - 1)
    ffn = lambda j: jnp.maximum(j - 1, 0)
    tok = lambda w, t: pl.BlockSpec((None, tm, w), lambda j: (t(j) // nt, t(j) % nt, 0))
    tr = pl.BlockSpec((None, D_MLSTM, tm), lambda j: (mixed(j) // nt, 0, mixed(j) % nt))
    shared = mod.shape[0] == 1
    mod_of = lambda t: pl.BlockSpec((None, N_MOD, d), lambda j: (0 if shared else t(j) // nt, 0, 0))
    hnw_b = jnp.broadcast_to(hnw.reshape(D_MLSTM, 1), (D_MLSTM, tm))
    return pl.pallas_call(
        _combine_ffn_kernel,
        grid=(n_tiles + 1,),
        in_specs=[tok(d, mixed), tok(D_FOURIER, mixed), tr, tr, tr, mod_of(mixed), mod_of(ffn),
                  _resident(norm_w.shape, (l,)), _resident(hnw_b.shape), _resident(w_mix.shape, (l,)),
                  _resident(ff_in.shape, (l, 1)), _resident(ff_out.shape, (l, 1))],
        out_specs=tok(d, ffn),
        out_shape=jax.ShapeDtypeStruct(x.shape, F32),
        scratch_shapes=[pltpu.VMEM((tm, D_FF), BF16),
                        pltpu.VMEM((tm, d), F32), pltpu.VMEM((tm, d), BF16),
                        pltpu.VMEM((tm, d), F32), pltpu.VMEM((tm, d), BF16)],
        compiler_params=_cparams(1, 60),
        name="combine_ffn",
    )(x, yf, htf, htb, zt, mod, mod, norm_w, hnw_b, w_mix, ff_in, ff_out)


def _gate_layout(w_gate, gate_b):
    h = MLSTM_HEADS
    wt = w_gate.T
    w = jnp.concatenate([wt[0:h], wt[2 * h:3 * h], wt[h:2 * h], wt[3 * h:4 * h]], axis=0)
    bias = jnp.concatenate([gate_b[0], gate_b[2], gate_b[1], gate_b[3]])
    return w, jnp.broadcast_to(bias.reshape(GATE_ROWS, 1), (GATE_ROWS, CHUNK))


def kernel(x, c, ctx, c_ctx, w_ada, b_ada, norm_w, w_ff_in, w_ff_out, w_in, w_fmix,
           conv_w, conv_b, w_qkv, gate_b, mlstm_norm_w, w_out):
    b = x.shape[0]
    d = D_MODEL
    n_cond = b + 1
    assert n_cond <= SUBLANES
    cc = jnp.concatenate([c, c_ctx[None], jnp.zeros((SUBLANES - n_cond, d), F32)], axis=0)
    mod = _modulation(cc, w_ada, b_ada).reshape(DEPTH, SUBLANES, N_MOD, d)
    g_mix = _fold_fourier(w_fmix)
    w_fm = w_in[:, :, :D_FOURIER + D_MLSTM].astype(BF16)
    ff_in = w_ff_in.astype(BF16)
    ff_out = w_ff_out.astype(BF16)
    w_mix = w_out.astype(BF16)

    xl, xc = x, ctx
    c_zero = jnp.zeros((b, N_HD, STATE_ROWS, MLSTM_HD), F32)
    m_zero = jnp.zeros((b, N_HD, CHUNK), F32)
    for l in range(DEPTH):
        last = l == DEPTH - 1
        mod_l = mod[l, :b]
        mod_c = mod[l, b:b + 1]
        o_z = D_FOURIER + D_MLSTM
        o_g = D_FOURIER + 2 * D_MLSTM
        w_gate, g_bias = _gate_layout(w_in[l, :, o_g:], gate_b[l])
        pw = (w_fm, g_mix, jnp.concatenate([w_in[l, :, o_z:o_g].T, w_gate], axis=0).astype(BF16))
        cw = conv_w[l].reshape(9, D_MLSTM)
        cb = conv_b[l].reshape(1, D_MLSTM)
        qkvw = (jnp.swapaxes(w_qkv[l, 0], 1, 2).astype(BF16), w_qkv[l, 1].astype(BF16),
                jnp.swapaxes(w_qkv[l, 2], 1, 2).astype(BF16))
        hnw = mlstm_norm_w[l]

        xl, z_l, xm_l, zt_l, gt_l = _ffn_inproj(xl, mod_l, l, norm_w, ff_in, ff_out, *pw)
        xc, z_c, xm_c, zt_c, gt_c = _ffn_inproj(xc, mod_c, l, norm_w, ff_in, ff_out, *pw)
        qkv_l = _convqkv(xm_l, GRID_W, cw, cb, *qkvw)
        qkv_c = _convqkv(xm_c, xm_c.shape[1], cw, cb, *qkvw)
        htf_c, htb_c, c_ctx_st, m_ctx_st = _mlstm(*qkv_c, gt_c, g_bias, c_zero, m_zero)
        htf_l, htb_l, _, _ = _mlstm(*qkv_l, gt_l, g_bias, c_ctx_st, m_ctx_st)
        yf_l = _seq_dft(z_l)
        xl = _combine_ffn(xl, yf_l, htf_l, htb_l, zt_l, mod_l, l, norm_w, hnw, w_mix, ff_in, ff_out)
        if not last:
            yf_c = _dense_dft(z_c)
            xc = _combine_ffn(xc, yf_c, htf_c, htb_c, zt_c, mod_c, l, norm_w, hnw, w_mix, ff_in, ff_out)
    return xl
```

```python
import functools
import math

import numpy as np
import jax
import jax.numpy as jnp
from jax import lax
from jax.experimental import pallas as pl
from jax.experimental.pallas import tpu as pltpu

D_MODEL = 1024
DEPTH = 2
GRID_W = 64
D_FOURIER = 512
FOURIER_GROUPS = 8
FOURIER_GW = 64
D_MLSTM = 512
MLSTM_HEADS = 4
MLSTM_HD = 128
CHUNK = 128
D_FF = 2816
N_MOD = 9
FFN_RES = 0.5
EPS = 1e-6

LANES = 128
SUBLANES = 8
PACKED_SUBLANES = 16
MXU_DIM = 256
VMEM_BYTES = 64 * 1024 * 1024
N_HD = 2 * MLSTM_HEADS
STATE_ROWS = MLSTM_HD + PACKED_SUBLANES
GATE_ROWS = 2 * N_HD
SCAN_CHUNKS = 8

F32 = jnp.float32
BF16 = jnp.bfloat16
NT = (((1,), (1,)), ((), ()))
TN = (((0,), (0,)), ((), ()))


def _cparams(n_grid, vmem_mb):
    return pltpu.CompilerParams(
        dimension_semantics=("arbitrary",) * n_grid,
        vmem_limit_bytes=min(vmem_mb * 1024 * 1024, VMEM_BYTES - 4 * 1024 * 1024))


def _dot(a, b):
    return jnp.dot(a, b, preferred_element_type=F32)


def _dot_nt(a, b):
    return lax.dot_general(a, b, NT, preferred_element_type=F32)


def _rms(x, w):
    return x * lax.rsqrt(jnp.mean(x * x, axis=-1, keepdims=True) + EPS) * w


def _mod3(m, j):
    return m[3 * j:3 * j + 1], m[3 * j + 1:3 * j + 2], m[3 * j + 2:3 * j + 3]


def _mod_index(mod):
    if mod.shape[0] == 1:
        return lambda b, i: (0, 0, 0)
    return lambda b, i: (b, 0, 0)


def _full(shape):
    return pl.BlockSpec(shape, lambda *_: (0,) * len(shape))


def _resident(shape, lead=()):
    rest = tuple(shape[len(lead):])
    index = tuple(lead) + (0,) * len(rest)
    return pl.BlockSpec((None,) * len(lead) + rest, lambda *_: index, pipeline_mode=pl.Buffered(1))


def _mod_kernel(c_ref, w_ref, b_ref, o_ref):
    c = c_ref[...]
    s = (c * jax.nn.sigmoid(c)).astype(BF16)
    o_ref[...] = _dot(s, w_ref[...].astype(BF16)) + b_ref[...]


def _modulation(cc, w_ada, b_ada):
    d = D_MODEL
    return pl.pallas_call(
        _mod_kernel,
        grid=(DEPTH, N_MOD),
        in_specs=[pl.BlockSpec((SUBLANES, d), lambda l, n: (0, 0)),
                  pl.BlockSpec((None, d, d), lambda l, n: (l, 0, n)),
                  pl.BlockSpec((None, 1, d), lambda l, n: (l, 0, n))],
        out_specs=pl.BlockSpec((None, SUBLANES, d), lambda l, n: (l, 0, n)),
        out_shape=jax.ShapeDtypeStruct((DEPTH, SUBLANES, N_MOD * d), F32),
        compiler_params=_cparams(2, 32),
        name="modulation",
    )(cc, w_ada, b_ada.reshape(DEPTH, 1, N_MOD * d))


FF_CHUNK = MXU_DIM
N_FF_CHUNKS = D_FF // FF_CHUNK


def _ffn_prenorm(x, mod, j, nw_ref):
    shift, scale, _ = _mod3(mod, j)
    return (_rms(x, nw_ref[2 * j:2 * j + 1]) * (1.0 + scale) + shift).astype(BF16)


def _ffn_chunks(h, wi_ref, a_ref, lo, hi):
    for c in range(lo, hi):
        sl = slice(c * FF_CHUNK, (c + 1) * FF_CHUNK)
        g = _dot(h, wi_ref[:, sl])
        u = _dot(h, wi_ref[:, D_FF + c * FF_CHUNK:D_FF + (c + 1) * FF_CHUNK])
        a_ref[:, sl] = (g * jax.nn.sigmoid(g) * u).astype(BF16)


def _ffn_finish(x, mod, j, nw_ref, wo_ref, a_ref):
    _, _, gate = _mod3(mod, j)
    y = _dot(a_ref[...], wo_ref[...])
    return x + FFN_RES * gate * _rms(y, nw_ref[2 * j + 1:2 * j + 2])


def _ffn_body(x, mod, j, nw_ref, wi_ref, wo_ref, a_ref):
    _ffn_chunks(_ffn_prenorm(x, mod, j, nw_ref), wi_ref, a_ref, 0, N_FF_CHUNKS)
    return _ffn_finish(x, mod, j, nw_ref, wo_ref, a_ref)


def _fold_kernel(bdw_ref, bdc_ref, bds_ref, o_ref):
    hi = lax.Precision.HIGHEST
    bdw = bdw_ref[...]
    t = MXU_DIM
    for part, dft_ref in enumerate((bdc_ref, bds_ref)):
        g = jnp.dot(dft_ref[...], bdw, precision=hi, preferred_element_type=F32)
        for i in range(D_FOURIER // t):
            o_ref[part * (D_FOURIER // t) + i] = g[i * t:(i + 1) * t, i * t:(i + 1) * t].astype(BF16)


def _channel_dft_blocks():
    c = np.arange(FOURIER_GW, dtype=np.float64)
    ang = 2.0 * np.pi * np.outer(c, c) / FOURIER_GW
    eye = np.eye(FOURIER_GROUPS)
    scale = 1.0 / math.sqrt(FOURIER_GW)
    bdc = np.kron(eye, np.cos(ang) * scale)
    bds = np.kron(eye, -np.sin(ang) * scale)
    return jnp.asarray(bdc, F32), jnp.asarray(bds, F32)


N_MIX_TILES = 2 * D_FOURIER // MXU_DIM


def _fold_fourier(w_fmix):
    eye = jnp.eye(FOURIER_GROUPS, dtype=F32)
    bdw = jnp.einsum("gh,lgcd->lgchd", eye, w_fmix).reshape(DEPTH, D_FOURIER, D_FOURIER)
    bdc, bds = _channel_dft_blocks()
    sq = pl.BlockSpec((D_FOURIER, D_FOURIER), lambda l: (0, 0))
    return pl.pallas_call(
        _fold_kernel,
        grid=(DEPTH,),
        in_specs=[pl.BlockSpec((None, D_FOURIER, D_FOURIER), lambda l: (l, 0, 0)), sq, sq],
        out_specs=pl.BlockSpec((None, N_MIX_TILES, MXU_DIM, MXU_DIM), lambda l: (l, 0, 0, 0)),
        out_shape=jax.ShapeDtypeStruct((DEPTH, N_MIX_TILES, MXU_DIM, MXU_DIM), BF16),
        compiler_params=_cparams(1, 32),
        name="fold_fourier",
    )(bdw, bdc, bds)


def _ffn_inproj_kernel(x_ref, mod_ref, nw_ref, wi_ref, wo_ref, wfm_ref, g_ref, wt_ref,
                       o_ref, z_ref, xm_ref, zt_ref, gt_ref, a_ref):
    mod = mod_ref[...]
    x1 = _ffn_body(x_ref[...], mod, 0, nw_ref, wi_ref, wo_ref, a_ref)
    o_ref[...] = x1
    shift, scale, _ = _mod3(mod, 1)
    h = (_rms(x1, nw_ref[2:3]) * (1.0 + scale) + shift).astype(BF16)
    u = _dot(h, wfm_ref[...])
    xm_ref[...] = u[:, D_FOURIER:]
    xf = u[:, :D_FOURIER].astype(BF16)
    t = MXU_DIM
    n_src = D_FOURIER // t
    for i in range(N_MIX_TILES):
        src = xf[:, (i % n_src) * t:(i % n_src + 1) * t]
        z_ref[:, i * t:(i + 1) * t] = _dot(src, g_ref[i])
    tt = _dot_nt(wt_ref[...], h)
    zt_ref[...] = tt[:D_MLSTM].astype(zt_ref.dtype)
    gt_ref[...] = tt[D_MLSTM:]


def _ffn_inproj(x, mod, l, norm_w, ff_in, ff_out, wfm, g_mix, wt):
    b, s, d = x.shape
    tm = min(1024, s)
    tok = lambda w: pl.BlockSpec((None, tm, w), lambda bi, i: (bi, i, 0))
    tr = lambda r: pl.BlockSpec((None, r, tm), lambda bi, i: (bi, 0, i))
    return pl.pallas_call(
        _ffn_inproj_kernel,
        grid=(b, s // tm),
        in_specs=[tok(d), pl.BlockSpec((None, N_MOD, d), _mod_index(mod)), _resident(norm_w.shape, (l,)),
                  _resident(ff_in.shape, (l, 0)), _resident(ff_out.shape, (l, 0)),
                  _resident(wfm.shape, (l,)), _resident(g_mix.shape, (l,)), _resident(wt.shape)],
        out_specs=[tok(d), tok(2 * D_FOURIER), tok(D_MLSTM), tr(D_MLSTM), tr(GATE_ROWS)],
        out_shape=[jax.ShapeDtypeStruct(x.shape, F32),
                   jax.ShapeDtypeStruct((b, s, 2 * D_FOURIER), F32),
                   jax.ShapeDtypeStruct((b, s, D_MLSTM), F32),
                   jax.ShapeDtypeStruct((b, D_MLSTM, s), BF16),
                   jax.ShapeDtypeStruct((b, GATE_ROWS, s), F32)],
        scratch_shapes=[pltpu.VMEM((tm, D_FF), BF16)],
        compiler_params=_cparams(2, 60),
        name="ffn_inproj",
    )(x, mod, norm_w, ff_in, ff_out, wfm, g_mix, wt)


def _convqkv_kernel(width, rows, halo, prev_ref, cur_ref, next_ref, cw_ref, cb_ref,
                    wqt_ref, wk_ref, wvt_ref, qt_ref, k_ref, vt_ref):
    i = pl.program_id(1)
    tm = cur_ref.shape[0]
    cur = cur_ref[...]
    prev = jnp.where(i > 0, prev_ref[...], 0.0)
    nxt = jnp.where(i < pl.num_programs(1) - 1, next_ref[...], 0.0)
    ext = jnp.concatenate([prev, cur, nxt], axis=0)
    n = ext.shape[0]
    r = lax.broadcasted_iota(jnp.int32, ext.shape, 0)
    col = (r + (width - halo % width)) & (width - 1)
    left = jnp.where(col == 0, 0.0, pltpu.roll(ext, 1, axis=0))
    right = jnp.where(col == width - 1, 0.0, pltpu.roll(ext, n - 1, axis=0))
    taps = (left, ext, right)
    cw = cw_ref[...]
    acc = jnp.zeros((tm, cur.shape[1]), F32) + cb_ref[...]
    for dy in (-1, 0, 1):
        if rows == 1 and dy != 0:
            continue
        lo = halo + dy * width
        for dx in (-1, 0, 1):
            k = (dy + 1) * 3 + (dx + 1)
            acc = acc + cw[k:k + 1] * taps[dx + 1][lo:lo + tm]
    cv = (acc * jax.nn.sigmoid(acc)).astype(BF16)
    xm = cur.astype(BF16)
    hd = MLSTM_HD
    for h in range(MLSTM_HEADS):
        sl = slice(h * hd, (h + 1) * hd)
        qt_ref[sl, :] = _dot_nt(wqt_ref[h], cv[:, sl]).astype(BF16)
        k_ref[:, sl] = (_dot(cv[:, sl], wk_ref[h]) * hd ** -0.5).astype(BF16)
        vt_ref[sl, :] = _dot_nt(wvt_ref[h], xm[:, sl]).astype(BF16)


def _convqkv(xm, width, cw, cb, wqt, wk, wvt):
    b, s, c = xm.shape
    rows = s // width
    tm = min(512, s)
    halo = LANES if rows > 1 else SUBLANES
    assert tm % width == 0 and tm % halo == 0 and (rows == 1 or halo > width)
    assert width & (width - 1) == 0
    per = tm // halo
    nblk = s // halo
    kern = functools.partial(_convqkv_kernel, width, rows, halo)
    tr = pl.BlockSpec((None, c, tm), lambda bi, i: (bi, 0, i))
    return pl.pallas_call(
        kern,
        grid=(b, s // tm),
        in_specs=[pl.BlockSpec((None, halo, c), lambda bi, i: (bi, jnp.maximum(i * per - 1, 0), 0)),
                  pl.BlockSpec((None, tm, c), lambda bi, i: (bi, i, 0)),
                  pl.BlockSpec((None, halo, c), lambda bi, i: (bi, jnp.minimum((i + 1) * per, nblk - 1), 0)),
                  _full(cw.shape), _full(cb.shape), _full(wqt.shape), _full(wk.shape), _full(wvt.shape)],
        out_specs=[tr, pl.BlockSpec((None, tm, c), lambda bi, i: (bi, i, 0)), tr],
        out_shape=[jax.ShapeDtypeStruct((b, c, s), BF16),
                   jax.ShapeDtypeStruct((b, s, c), BF16),
                   jax.ShapeDtypeStruct((b, c, s), BF16)],
        compiler_params=_cparams(2, 32),
        name="convqkv",
    )(xm, xm, xm, cw, cb, wqt, wk, wvt)


def _tri_ones():
    u = np.arange(CHUNK)[:, None]
    t = np.arange(CHUNK)[None, :]
    return jnp.asarray(np.concatenate([u <= t, u >= t], axis=0), BF16)


def _split_dot(x, w):
    hi = x.astype(BF16)
    r1 = x - hi.astype(F32)
    mid = r1.astype(BF16)
    lo = (r1 - mid.astype(F32)).astype(BF16)
    return _dot(hi, w) + _dot(mid, w) + _dot(lo, w)


def _mlstm_kernel(qtf_ref, kf_ref, vtf_ref, gtf_ref, qtb_ref, kb_ref, vtb_ref, gtb_ref,
                  bias_ref, tri_ref, c0_ref, m0_ref, htf_ref, htb_ref, cfin_ref, mfin_ref, c_st, m_st):
    j = pl.program_id(1)
    L = CHUNK
    hd = MLSTM_HD
    n_sub = gtf_ref.shape[1] // L

    @pl.when(j == 0)
    def _():
        c_st[...] = c0_ref[...]
        m_st[...] = m0_ref[...]

    row_fwd = lax.broadcasted_iota(jnp.int32, (N_HD, L), 0) < MLSTM_HEADS
    src_row = lax.broadcasted_iota(jnp.int32, (L, L), 0)
    out_col = lax.broadcasted_iota(jnp.int32, (L, L), 1)
    one_row = jnp.where(lax.broadcasted_iota(jnp.int32, (PACKED_SUBLANES, L), 0) == 0, 1.0, 0.0).astype(BF16)
    bcast = lambda col: jnp.broadcast_to(col, (N_HD, L))

    def operands(u, p):
        d, hh = divmod(p, MLSTM_HEADS)
        hs = slice(hh * hd, (hh + 1) * hd)
        if d == 0:
            tl = slice(u * L, (u + 1) * L)
            return qtf_ref[hs, tl], kf_ref[tl, hs], vtf_ref[hs, tl], htf_ref, hs, tl
        tl = slice((n_sub - 1 - u) * L, (n_sub - u) * L)
        return qtb_ref[hs, tl], kb_ref[tl, hs], vtb_ref[hs, tl], htb_ref, hs, tl

    gates = []
    for u in range(n_sub):
        fl = slice(u * L, (u + 1) * L)
        bl = slice((n_sub - 1 - u) * L, (n_sub - u) * L)
        ig = jnp.where(row_fwd, gtf_ref[:N_HD, fl], gtb_ref[:N_HD, bl]) + bias_ref[:N_HD]
        fp = jnp.where(row_fwd, gtf_ref[N_HD:, fl], gtb_ref[N_HD:, bl]) + bias_ref[N_HD:]
        lf = jax.nn.log_sigmoid(fp)
        lf2 = jnp.concatenate([jnp.where(row_fwd, lf, 0.0), jnp.where(row_fwd, 0.0, lf)], axis=1)
        bc = _split_dot(lf2, tri_ref[...])
        gs = ig - bc
        btot = jnp.where(row_fwd, bcast(bc[:, L - 1:L]), bcast(bc[:, 0:1]))
        gmax = bcast(jnp.max(gs, axis=1, keepdims=True))
        gates.append((bc, gs, btot, gmax, jnp.exp(gs - gmax)))

    scores, local = {}, {}
    for u in range(n_sub):
        w_st = gates[u][4]
        for p in range(N_HD):
            qt, k, vt, _, _, _ = operands(u, p)
            scores[u, p] = _dot(k, qt)
            vaug = jnp.concatenate([vt, one_row], axis=0)
            vw = (vaug.astype(F32) * w_st[p:p + 1, :]).astype(BF16)
            local[u, p] = _dot(vw, k)

    for u in range(n_sub):
        bc, gs, btot, gmax, _ = gates[u]
        m_prev = m_st[...]
        m_loc = btot + gmax
        m_new = jnp.maximum(btot + m_prev, m_loc)
        dec = jnp.exp(btot + m_prev - m_new)
        inj = jnp.exp(m_loc - m_new)
        m_st[...] = m_new
        for p in range(N_HD):
            qt, k, vt, dst, hs, tl = operands(u, p)
            seen = (src_row <= out_col) if p < MLSTM_HEADS else (src_row >= out_col)
            gs_seen = jnp.where(seen, jnp.broadcast_to(gs[p:p + 1, :], (L, L)).T, -jnp.inf)
            xm = jnp.maximum(m_prev[p:p + 1, :], jnp.max(gs_seen, axis=0, keepdims=True))
            pt = (scores[u, p] * jnp.exp(gs_seen - xm)).astype(BF16)
            qw = (qt.astype(F32) * jnp.exp(m_prev[p:p + 1, :] - xm)).astype(BF16)
            vaug = jnp.concatenate([vt, one_row], axis=0)
            c_prev = c_st[p]
            lhs = jnp.concatenate([vaug, c_prev.astype(BF16)], axis=1)
            rhs = jnp.concatenate([pt, qw], axis=0)
            o = _dot(lhs, rhs)
            den = jnp.maximum(jnp.abs(o[hd:hd + 1]), jnp.exp(-(bc[p:p + 1, :] + xm)))
            dst[hs, tl] = (o[:hd] * (1.0 / den)).astype(dst.dtype)
            c_st[p] = dec[p:p + 1, :] * c_prev + inj[p:p + 1, :] * local[u, p]

    @pl.when(j == pl.num_programs(1) - 1)
    def _():
        cfin_ref[...] = c_st[...]
        mfin_ref[...] = m_st[...]


def _mlstm(qt, k, vt, gt, bias, c0, m0):
    b, c, s = qt.shape
    tb = min(SCAN_CHUNKS * CHUNK, s)
    nb = s // tb
    tri = _tri_ones()
    fwd_t = lambda bi, j: (bi, 0, j)
    bwd_t = lambda bi, j: (bi, 0, nb - 1 - j)
    fwd_n = lambda bi, j: (bi, j, 0)
    bwd_n = lambda bi, j: (bi, nb - 1 - j, 0)
    st_c = pl.BlockSpec((None, N_HD, STATE_ROWS, MLSTM_HD), lambda bi, j: (bi, 0, 0, 0))
    st_m = pl.BlockSpec((None, N_HD, CHUNK), lambda bi, j: (bi, 0, 0))
    side = lambda ti, ni: [pl.BlockSpec((None, c, tb), ti), pl.BlockSpec((None, tb, c), ni),
                           pl.BlockSpec((None, c, tb), ti), pl.BlockSpec((None, GATE_ROWS, tb), ti)]
    return pl.pallas_call(
        _mlstm_kernel,
        grid=(b, nb),
        in_specs=side(fwd_t, fwd_n) + side(bwd_t, bwd_n) + [_full(bias.shape), _full(tri.shape), st_c, st_m],
        out_specs=[pl.BlockSpec((None, c, tb), fwd_t), pl.BlockSpec((None, c, tb), bwd_t), st_c, st_m],
        out_shape=[jax.ShapeDtypeStruct((b, c, s), BF16),
                   jax.ShapeDtypeStruct((b, c, s), BF16),
                   jax.ShapeDtypeStruct(c0.shape, F32),
                   jax.ShapeDtypeStruct(m0.shape, F32)],
        scratch_shapes=[pltpu.VMEM((N_HD, STATE_ROWS, MLSTM_HD), F32),
                        pltpu.VMEM((N_HD, CHUNK), F32)],
        compiler_params=_cparams(2, 56),
        name="mlstm",
    )(qt, k, vt, gt, qt, k, vt, gt, bias, tri, c0, m0)


DFT_GROUP = SUBLANES


def _dft_tables(s, n2):
    n1 = s // n2
    k1 = np.arange(n1, dtype=np.float64)
    s1 = np.arange(n1, dtype=np.float64)
    ma = np.zeros((n2, 2 * n1, 2 * n1), np.float64)
    for s2 in range(n2):
        th = 2.0 * np.pi * np.outer(k1, s1 * n2 + s2) / s
        c, sn = np.cos(th), np.sin(th)
        ma[s2] = np.block([[c, sn], [-sn, c]])
    k2 = np.arange(n2, dtype=np.float64)
    ph = 2.0 * np.pi * np.outer(k2, k2) / n2
    eye = np.eye(DFT_GROUP)
    scale = 1.0 / math.sqrt(s)
    lb = np.einsum("ksr,ab->kasrb", np.stack([np.cos(ph), np.sin(ph)], axis=-1) * scale, eye)
    lb = lb.reshape(n2 * DFT_GROUP, n2 * 2 * DFT_GROUP)
    return jnp.asarray(ma, F32).astype(BF16), jnp.asarray(lb, F32).astype(BF16)


def _seq_dft_kernel(n_a, z_ref, ma_ref, lb_ref, y_ref, t_scr):
    j = pl.program_id(1)
    g = DFT_GROUP
    n2 = t_scr.shape[0]
    n1 = t_scr.shape[2]

    @pl.when(j < n_a)
    def _():
        for s in range(g):
            zs = z_ref[:, s, :]
            rhs = jnp.concatenate([zs[:, :D_FOURIER], zs[:, D_FOURIER:]], axis=0).astype(BF16)
            t_scr[j * g + s] = _dot(ma_ref[s], rhs).reshape(2, n1, D_FOURIER)

    @pl.when(j >= n_a)
    def _():
        parts = []
        for q in range(y_ref.shape[1] // g):
            k0 = pl.multiple_of((j - n_a) * y_ref.shape[1] + q * g, g)
            rhs = t_scr[:, :, pl.ds(k0, g), :].reshape(n2 * 2 * g, D_FOURIER).astype(BF16)
            parts.append(_dot(lb_ref[...], rhs).reshape(n2, g, D_FOURIER))
        y_ref[...] = jnp.concatenate(parts, axis=1).astype(y_ref.dtype)


def _seq_dft(z, n2=64):
    b, s, w = z.shape
    n1 = s // n2
    g = DFT_GROUP
    n_a = n2 // g
    gb = PACKED_SUBLANES
    n_b = n1 // gb
    ma, lb = _dft_tables(s, n2)
    y = pl.pallas_call(
        functools.partial(_seq_dft_kernel, n_a),
        grid=(b, n_a + n_b),
        in_specs=[pl.BlockSpec((None, n1, g, w), lambda bi, j: (bi, 0, jnp.minimum(j, n_a - 1), 0)),
                  pl.BlockSpec((g, 2 * n1, 2 * n1), lambda bi, j: (jnp.minimum(j, n_a - 1), 0, 0)),
                  _full(lb.shape)],
        out_specs=pl.BlockSpec((None, n2, gb, D_FOURIER), lambda bi, j: (bi, 0, jnp.maximum(j - n_a, 0), 0)),
        out_shape=jax.ShapeDtypeStruct((b, n2, n1, D_FOURIER), BF16),
        scratch_shapes=[pltpu.VMEM((n2, 2, n1, D_FOURIER), F32)],
        compiler_params=_cparams(2, 60),
        name="seq_dft",
    )(z.reshape(b, n1, n2, w), ma, lb)
    return y.reshape(b, s, D_FOURIER)


def _dense_dft_kernel(z_ref, ld_ref, y_ref):
    rhs = jnp.concatenate([z_ref[:, :D_FOURIER], z_ref[:, D_FOURIER:]], axis=0).astype(BF16)
    y_ref[...] = _dot(ld_ref[...], rhs).astype(y_ref.dtype)


def _dense_dft(z):
    b, s, w = z.shape
    k = np.arange(s, dtype=np.float64)
    th = 2.0 * np.pi * np.outer(k, k) / s
    ld = jnp.asarray(np.concatenate([np.cos(th), np.sin(th)], axis=1) / math.sqrt(s), F32).astype(BF16)
    return pl.pallas_call(
        _dense_dft_kernel,
        grid=(b,),
        in_specs=[pl.BlockSpec((None, s, w), lambda bi: (bi, 0, 0)), _full(ld.shape)],
        out_specs=pl.BlockSpec((None, s, D_FOURIER), lambda bi: (bi, 0, 0)),
        out_shape=jax.ShapeDtypeStruct((b, s, D_FOURIER), BF16),
        compiler_params=_cparams(1, 32),
        name="dft_dense",
    )(z, ld)


def _mixer_out(x_ref, yf_ref, htf_ref, htb_ref, zt_ref, mod, nw_ref, hnw_ref, wom_ref):
    _, _, gate = _mod3(mod, 1)
    hd = MLSTM_HD
    hs = htf_ref[...].astype(F32) + htb_ref[...].astype(F32)
    parts = []
    for h in range(MLSTM_HEADS):
        seg = hs[h * hd:(h + 1) * hd]
        mu = jnp.mean(seg, axis=0, keepdims=True)
        cen = seg - mu
        var = jnp.mean(cen * cen, axis=0, keepdims=True)
        parts.append(cen * lax.rsqrt(var + EPS))
    hn = jnp.concatenate(parts, axis=0) * hnw_ref[...]
    ymt = (hn * jax.nn.sigmoid(zt_ref[...].astype(F32))).astype(BF16)
    y = (_dot(yf_ref[...], wom_ref[:D_FOURIER])
         + lax.dot_general(ymt, wom_ref[D_FOURIER:], TN, preferred_element_type=F32))
    return x_ref[...] + gate * _rms(y, nw_ref[3:4])


def _combine_ffn_kernel(x_ref, yf_ref, htf_ref, htb_ref, zt_ref, mod_ref, nw_ref, hnw_ref, wom_ref,
                        wi_ref, wo_ref, o_ref, a_ref):
    mod = mod_ref[...]
    x2 = _mixer_out(x_ref, yf_ref, htf_ref, htb_ref, zt_ref, mod, nw_ref, hnw_ref, wom_ref)
    o_ref[...] = _ffn_body(x2, mod, 2, nw_ref, wi_ref, wo_ref, a_ref)


def _combine_ffn(x, yf, htf, htb, zt, mod, l, norm_w, hnw, w_mix, ff_in, ff_out):
    b, s, d = x.shape
    tm = min(1024, s)
    tok = lambda w: pl.BlockSpec((None, tm, w), lambda bi, i: (bi, i, 0))
    tr = pl.BlockSpec((None, D_MLSTM, tm), lambda bi, i: (bi, 0, i))
    hnw_b = jnp.broadcast_to(hnw.reshape(D_MLSTM, 1), (D_MLSTM, tm))
    return pl.pallas_call(
        _combine_ffn_kernel,
        grid=(b, s // tm),
        in_specs=[tok(d), tok(D_FOURIER), tr, tr, tr,
                  pl.BlockSpec((None, N_MOD, d), _mod_index(mod)),
                  _resident(norm_w.shape, (l,)), _resident(hnw_b.shape), _resident(w_mix.shape, (l,)),
                  _resident(ff_in.shape, (l, 1)), _resident(ff_out.shape, (l, 1))],
        out_specs=tok(d),
        out_shape=jax.ShapeDtypeStruct(x.shape, F32),
        scratch_shapes=[pltpu.VMEM((tm, D_FF), BF16)],
        compiler_params=_cparams(2, 60),
        name="combine_ffn",
    )(x, yf, htf, htb, zt, mod, norm_w, hnw_b, w_mix, ff_in, ff_out)


def _gate_layout(w_gate, gate_b):
    h = MLSTM_HEADS
    wt = w_gate.T
    w = jnp.concatenate([wt[0:h], wt[2 * h:3 * h], wt[h:2 * h], wt[3 * h:4 * h]], axis=0)
    bias = jnp.concatenate([gate_b[0], gate_b[2], gate_b[1], gate_b[3]])
    return w, jnp.broadcast_to(bias.reshape(GATE_ROWS, 1), (GATE_ROWS, CHUNK))


def kernel(x, c, ctx, c_ctx, w_ada, b_ada, norm_w, w_ff_in, w_ff_out, w_in, w_fmix,
           conv_w, conv_b, w_qkv, gate_b, mlstm_norm_w, w_out):
    b = x.shape[0]
    d = D_MODEL
    n_cond = b + 1
    assert n_cond <= SUBLANES
    cc = jnp.concatenate([c, c_ctx[None], jnp.zeros((SUBLANES - n_cond, d), F32)], axis=0)
    mod = _modulation(cc, w_ada, b_ada).reshape(DEPTH, SUBLANES, N_MOD, d)
    g_mix = _fold_fourier(w_fmix)
    w_fm = w_in[:, :, :D_FOURIER + D_MLSTM].astype(BF16)
    ff_in = w_ff_in.astype(BF16)
    ff_out = w_ff_out.astype(BF16)
    w_mix = w_out.astype(BF16)

    xl, xc = x, ctx
    c_zero = jnp.zeros((b, N_HD, STATE_ROWS, MLSTM_HD), F32)
    m_zero = jnp.zeros((b, N_HD, CHUNK), F32)
    for l in range(DEPTH):
        last = l == DEPTH - 1
        mod_l = mod[l, :b]
        mod_c = mod[l, b:b + 1]
        o_z = D_FOURIER + D_MLSTM
        o_g = D_FOURIER + 2 * D_MLSTM
        w_gate, g_bias = _gate_layout(w_in[l, :, o_g:], gate_b[l])
        pw = (w_fm, g_mix, jnp.concatenate([w_in[l, :, o_z:o_g].T, w_gate], axis=0).astype(BF16))
        cw = conv_w[l].reshape(9, D_MLSTM)
        cb = conv_b[l].reshape(1, D_MLSTM)
        qkvw = (jnp.swapaxes(w_qkv[l, 0], 1, 2).astype(BF16), w_qkv[l, 1].astype(BF16),
                jnp.swapaxes(w_qkv[l, 2], 1, 2).astype(BF16))
        hnw = mlstm_norm_w[l]

        xl, z_l, xm_l, zt_l, gt_l = _ffn_inproj(xl, mod_l, l, norm_w, ff_in, ff_out, *pw)
        xc, z_c, xm_c, zt_c, gt_c = _ffn_inproj(xc, mod_c, l, norm_w, ff_in, ff_out, *pw)
        qkv_l = _convqkv(xm_l, GRID_W, cw, cb, *qkvw)
        qkv_c = _convqkv(xm_c, xm_c.shape[1], cw, cb, *qkvw)
        htf_c, htb_c, c_ctx_st, m_ctx_st = _mlstm(*qkv_c, gt_c, g_bias, c_zero, m_zero)
        htf_l, htb_l, _, _ = _mlstm(*qkv_l, gt_l, g_bias, c_ctx_st, m_ctx_st)
        yf_l = _seq_dft(z_l)
        xl = _combine_ffn(xl, yf_l, htf_l, htb_l, zt_l, mod_l, l, norm_w, hnw, w_mix, ff_in, ff_out)
        if not last:
            yf_c = _dense_dft(z_c)
            xc = _combine_ffn(xc, yf_c, htf_c, htb_c, zt_c, mod_c, l, norm_w, hnw, w_mix, ff_in, ff_out)
    return xl
```

```python
import functools
import math

import numpy as np
import jax
import jax.numpy as jnp
from jax import lax
from jax.experimental import pallas as pl
from jax.experimental.pallas import tpu as pltpu

D_MODEL = 1024
DEPTH = 2
GRID_W = 64
D_FOURIER = 512
FOURIER_GROUPS = 8
FOURIER_GW = 64
D_MLSTM = 512
MLSTM_HEADS = 4
MLSTM_HD = 128
CHUNK = 128
D_FF = 2816
N_MOD = 9
FFN_RES = 0.5
EPS = 1e-6

LANES = 128
SUBLANES = 8
PACKED_SUBLANES = 16
MXU_DIM = 256
VMEM_BYTES = 64 * 1024 * 1024
N_HD = 2 * MLSTM_HEADS
STATE_ROWS = MLSTM_HD + PACKED_SUBLANES
GATE_ROWS = 2 * N_HD
SCAN_CHUNKS = 8

F32 = jnp.float32
BF16 = jnp.bfloat16
NT = (((1,), (1,)), ((), ()))
TN = (((0,), (0,)), ((), ()))


def _cparams(n_grid, vmem_mb):
    return pltpu.CompilerParams(
        dimension_semantics=("arbitrary",) * n_grid,
        vmem_limit_bytes=min(vmem_mb * 1024 * 1024, VMEM_BYTES - 4 * 1024 * 1024))


def _dot(a, b):
    return jnp.dot(a, b, preferred_element_type=F32)


def _dot_nt(a, b):
    return lax.dot_general(a, b, NT, preferred_element_type=F32)


def _rms(x, w):
    return x * lax.rsqrt(jnp.mean(x * x, axis=-1, keepdims=True) + EPS) * w


def _mod3(m, j):
    return m[3 * j:3 * j + 1], m[3 * j + 1:3 * j + 2], m[3 * j + 2:3 * j + 3]


def _mod_index(mod):
    if mod.shape[0] == 1:
        return lambda b, i: (0, 0, 0)
    return lambda b, i: (b, 0, 0)


def _full(shape):
    return pl.BlockSpec(shape, lambda *_: (0,) * len(shape))


def _resident(shape, lead=()):
    rest = tuple(shape[len(lead):])
    index = tuple(lead) + (0,) * len(rest)
    return pl.BlockSpec((None,) * len(lead) + rest, lambda *_: index, pipeline_mode=pl.Buffered(1))


def _mod_kernel(c_ref, w_ref, b_ref, o_ref):
    c = c_ref[...]
    s = (c * jax.nn.sigmoid(c)).astype(BF16)
    o_ref[...] = _dot(s, w_ref[...].astype(BF16)) + b_ref[...]


def _modulation(cc, w_ada, b_ada):
    d = D_MODEL
    return pl.pallas_call(
        _mod_kernel,
        grid=(DEPTH, N_MOD),
        in_specs=[pl.BlockSpec((SUBLANES, d), lambda l, n: (0, 0)),
                  pl.BlockSpec((None, d, d), lambda l, n: (l, 0, n)),
                  pl.BlockSpec((None, 1, d), lambda l, n: (l, 0, n))],
        out_specs=pl.BlockSpec((None, SUBLANES, d), lambda l, n: (l, 0, n)),
        out_shape=jax.ShapeDtypeStruct((DEPTH, SUBLANES, N_MOD * d), F32),
        compiler_params=_cparams(2, 32),
        name="modulation",
    )(cc, w_ada, b_ada.reshape(DEPTH, 1, N_MOD * d))


FF_CHUNK = MXU_DIM
N_FF_CHUNKS = D_FF // FF_CHUNK


def _ffn_prenorm(x, mod, j, nw_ref):
    shift, scale, _ = _mod3(mod, j)
    return (_rms(x, nw_ref[2 * j:2 * j + 1]) * (1.0 + scale) + shift).astype(BF16)


def _ffn_chunks(h, wi_ref, a_ref, lo, hi):
    for c in range(lo, hi):
        sl = slice(c * FF_CHUNK, (c + 1) * FF_CHUNK)
        g = _dot(h, wi_ref[:, sl])
        u = _dot(h, wi_ref[:, D_FF + c * FF_CHUNK:D_FF + (c + 1) * FF_CHUNK])
        a_ref[:, sl] = (g * jax.nn.sigmoid(g) * u).astype(BF16)


def _ffn_finish(x, mod, j, nw_ref, wo_ref, a_ref):
    _, _, gate = _mod3(mod, j)
    y = _dot(a_ref[...], wo_ref[...])
    return x + FFN_RES * gate * _rms(y, nw_ref[2 * j + 1:2 * j + 2])


def _ffn_body(x, mod, j, nw_ref, wi_ref, wo_ref, a_ref):
    _ffn_chunks(_ffn_prenorm(x, mod, j, nw_ref), wi_ref, a_ref, 0, N_FF_CHUNKS)
    return _ffn_finish(x, mod, j, nw_ref, wo_ref, a_ref)


def _fold_kernel(bdw_ref, bdc_ref, bds_ref, o_ref):
    hi = lax.Precision.HIGHEST
    bdw = bdw_ref[...]
    t = MXU_DIM
    for part, dft_ref in enumerate((bdc_ref, bds_ref)):
        g = jnp.dot(dft_ref[...], bdw, precision=hi, preferred_element_type=F32)
        for i in range(D_FOURIER // t):
            o_ref[part * (D_FOURIER // t) + i] = g[i * t:(i + 1) * t, i * t:(i + 1) * t].astype(BF16)


def _channel_dft_blocks():
    c = np.arange(FOURIER_GW, dtype=np.float64)
    ang = 2.0 * np.pi * np.outer(c, c) / FOURIER_GW
    eye = np.eye(FOURIER_GROUPS)
    scale = 1.0 / math.sqrt(FOURIER_GW)
    bdc = np.kron(eye, np.cos(ang) * scale)
    bds = np.kron(eye, -np.sin(ang) * scale)
    return jnp.asarray(bdc, F32), jnp.asarray(bds, F32)


N_MIX_TILES = 2 * D_FOURIER // MXU_DIM


def _fold_fourier(w_fmix):
    eye = jnp.eye(FOURIER_GROUPS, dtype=F32)
    bdw = jnp.einsum("gh,lgcd->lgchd", eye, w_fmix).reshape(DEPTH, D_FOURIER, D_FOURIER)
    bdc, bds = _channel_dft_blocks()
    sq = pl.BlockSpec((D_FOURIER, D_FOURIER), lambda l: (0, 0))
    return pl.pallas_call(
        _fold_kernel,
        grid=(DEPTH,),
        in_specs=[pl.BlockSpec((None, D_FOURIER, D_FOURIER), lambda l: (l, 0, 0)), sq, sq],
        out_specs=pl.BlockSpec((None, N_MIX_TILES, MXU_DIM, MXU_DIM), lambda l: (l, 0, 0, 0)),
        out_shape=jax.ShapeDtypeStruct((DEPTH, N_MIX_TILES, MXU_DIM, MXU_DIM), BF16),
        compiler_params=_cparams(1, 32),
        name="fold_fourier",
    )(bdw, bdc, bds)


def _ffn_inproj_kernel(x_ref, mod_ref, nw_ref, wi_ref, wo_ref, wfm_ref, g_ref, wt_ref,
                       o_ref, z_ref, xm_ref, zt_ref, gt_ref, a_ref):
    mod = mod_ref[...]
    x1 = _ffn_body(x_ref[...], mod, 0, nw_ref, wi_ref, wo_ref, a_ref)
    o_ref[...] = x1
    shift, scale, _ = _mod3(mod, 1)
    h = (_rms(x1, nw_ref[2:3]) * (1.0 + scale) + shift).astype(BF16)
    u = _dot(h, wfm_ref[...])
    xm_ref[...] = u[:, D_FOURIER:]
    xf = u[:, :D_FOURIER].astype(BF16)
    t = MXU_DIM
    n_src = D_FOURIER // t
    for i in range(N_MIX_TILES):
        src = xf[:, (i % n_src) * t:(i % n_src + 1) * t]
        z_ref[:, i * t:(i + 1) * t] = _dot(src, g_ref[i])
    tt = _dot_nt(wt_ref[...], h)
    zt_ref[...] = tt[:D_MLSTM].astype(zt_ref.dtype)
    gt_ref[...] = tt[D_MLSTM:]


def _ffn_inproj(x, mod, l, norm_w, ff_in, ff_out, wfm, g_mix, wt):
    b, s, d = x.shape
    tm = min(1024, s)
    tok = lambda w: pl.BlockSpec((None, tm, w), lambda bi, i: (bi, i, 0))
    tr = lambda r: pl.BlockSpec((None, r, tm), lambda bi, i: (bi, 0, i))
    return pl.pallas_call(
        _ffn_inproj_kernel,
        grid=(b, s // tm),
        in_specs=[tok(d), pl.BlockSpec((None, N_MOD, d), _mod_index(mod)), _resident(norm_w.shape, (l,)),
                  _resident(ff_in.shape, (l, 0)), _resident(ff_out.shape, (l, 0)),
                  _resident(wfm.shape, (l,)), _resident(g_mix.shape, (l,)), _resident(wt.shape)],
        out_specs=[tok(d), tok(2 * D_FOURIER), tok(D_MLSTM), tr(D_MLSTM), tr(GATE_ROWS)],
        out_shape=[jax.ShapeDtypeStruct(x.shape, F32),
                   jax.ShapeDtypeStruct((b, s, 2 * D_FOURIER), F32),
                   jax.ShapeDtypeStruct((b, s, D_MLSTM), F32),
                   jax.ShapeDtypeStruct((b, D_MLSTM, s), BF16),
                   jax.ShapeDtypeStruct((b, GATE_ROWS, s), F32)],
        scratch_shapes=[pltpu.VMEM((tm, D_FF), BF16)],
        compiler_params=_cparams(2, 60),
        name="ffn_inproj",
    )(x, mod, norm_w, ff_in, ff_out, wfm, g_mix, wt)


def _convqkv_kernel(width, rows, halo, prev_ref, cur_ref, next_ref, cw_ref, cb_ref,
                    wqt_ref, wk_ref, wvt_ref, qt_ref, k_ref, vt_ref):
    i = pl.program_id(1)
    tm = cur_ref.shape[0]
    cur = cur_ref[...]
    prev = jnp.where(i > 0, prev_ref[...], 0.0)
    nxt = jnp.where(i < pl.num_programs(1) - 1, next_ref[...], 0.0)
    ext = jnp.concatenate([prev, cur, nxt], axis=0)
    n = ext.shape[0]
    r = lax.broadcasted_iota(jnp.int32, ext.shape, 0)
    col = (r + (width - halo % width)) & (width - 1)
    left = jnp.where(col == 0, 0.0, pltpu.roll(ext, 1, axis=0))
    right = jnp.where(col == width - 1, 0.0, pltpu.roll(ext, n - 1, axis=0))
    taps = (left, ext, right)
    cw = cw_ref[...]
    acc = jnp.zeros((tm, cur.shape[1]), F32) + cb_ref[...]
    for dy in (-1, 0, 1):
        if rows == 1 and dy != 0:
            continue
        lo = halo + dy * width
        for dx in (-1, 0, 1):
            k = (dy + 1) * 3 + (dx + 1)
            acc = acc + cw[k:k + 1] * taps[dx + 1][lo:lo + tm]
    cv = (acc * jax.nn.sigmoid(acc)).astype(BF16)
    xm = cur.astype(BF16)
    hd = MLSTM_HD
    for h in range(MLSTM_HEADS):
        sl = slice(h * hd, (h + 1) * hd)
        qt_ref[sl, :] = _dot_nt(wqt_ref[h], cv[:, sl]).astype(BF16)
        k_ref[:, sl] = (_dot(cv[:, sl], wk_ref[h]) * hd ** -0.5).astype(BF16)
        vt_ref[sl, :] = _dot_nt(wvt_ref[h], xm[:, sl]).astype(BF16)


def _convqkv(xm, width, cw, cb, wqt, wk, wvt):
    b, s, c = xm.shape
    rows = s // width
    tm = min(2048, s)
    halo = LANES if rows > 1 else SUBLANES
    assert tm % width == 0 and tm % halo == 0 and (rows == 1 or halo > width)
    assert width & (width - 1) == 0
    per = tm // halo
    nblk = s // halo
    kern = functools.partial(_convqkv_kernel, width, rows, halo)
    tr = pl.BlockSpec((None, c, tm), lambda bi, i: (bi, 0, i))
    return pl.pallas_call(
        kern,
        grid=(b, s // tm),
        in_specs=[pl.BlockSpec((None, halo, c), lambda bi, i: (bi, jnp.maximum(i * per - 1, 0), 0)),
                  pl.BlockSpec((None, tm, c), lambda bi, i: (bi, i, 0)),
                  pl.BlockSpec((None, halo, c), lambda bi, i: (bi, jnp.minimum((i + 1) * per, nblk - 1), 0)),
                  _full(cw.shape), _full(cb.shape), _full(wqt.shape), _full(wk.shape), _full(wvt.shape)],
        out_specs=[tr, pl.BlockSpec((None, tm, c), lambda bi, i: (bi, i, 0)), tr],
        out_shape=[jax.ShapeDtypeStruct((b, c, s), BF16),
                   jax.ShapeDtypeStruct((b, s, c), BF16),
                   jax.ShapeDtypeStruct((b, c, s), BF16)],
        compiler_params=_cparams(2, 32),
        name="convqkv",
    )(xm, xm, xm, cw, cb, wqt, wk, wvt)


def _tri_ones():
    u = np.arange(CHUNK)[:, None]
    t = np.arange(CHUNK)[None, :]
    return jnp.asarray(np.concatenate([u <= t, u >= t], axis=0), BF16)


def _split_dot(x, w):
    hi = x.astype(BF16)
    r1 = x - hi.astype(F32)
    mid = r1.astype(BF16)
    lo = (r1 - mid.astype(F32)).astype(BF16)
    return _dot(hi, w) + _dot(mid, w) + _dot(lo, w)


def _mlstm_kernel(qtf_ref, kf_ref, vtf_ref, gtf_ref, qtb_ref, kb_ref, vtb_ref, gtb_ref,
                  bias_ref, tri_ref, c0_ref, m0_ref, htf_ref, htb_ref, cfin_ref, mfin_ref, c_st, m_st):
    j = pl.program_id(1)
    L = CHUNK
    hd = MLSTM_HD
    n_sub = gtf_ref.shape[1] // L

    @pl.when(j == 0)
    def _():
        c_st[...] = c0_ref[...]
        m_st[...] = m0_ref[...]

    row_fwd = lax.broadcasted_iota(jnp.int32, (N_HD, L), 0) < MLSTM_HEADS
    src_row = lax.broadcasted_iota(jnp.int32, (L, L), 0)
    out_col = lax.broadcasted_iota(jnp.int32, (L, L), 1)
    one_row = jnp.where(lax.broadcasted_iota(jnp.int32, (PACKED_SUBLANES, L), 0) == 0, 1.0, 0.0).astype(BF16)
    bcast = lambda col: jnp.broadcast_to(col, (N_HD, L))

    def operands(u, p):
        d, hh = divmod(p, MLSTM_HEADS)
        hs = slice(hh * hd, (hh + 1) * hd)
        if d == 0:
            tl = slice(u * L, (u + 1) * L)
            return qtf_ref[hs, tl], kf_ref[tl, hs], vtf_ref[hs, tl], htf_ref, hs, tl
        tl = slice((n_sub - 1 - u) * L, (n_sub - u) * L)
        return qtb_ref[hs, tl], kb_ref[tl, hs], vtb_ref[hs, tl], htb_ref, hs, tl

    gates = []
    for u in range(n_sub):
        fl = slice(u * L, (u + 1) * L)
        bl = slice((n_sub - 1 - u) * L, (n_sub - u) * L)
        ig = jnp.where(row_fwd, gtf_ref[:N_HD, fl], gtb_ref[:N_HD, bl]) + bias_ref[:N_HD]
        fp = jnp.where(row_fwd, gtf_ref[N_HD:, fl], gtb_ref[N_HD:, bl]) + bias_ref[N_HD:]
        lf = jax.nn.log_sigmoid(fp)
        lf2 = jnp.concatenate([jnp.where(row_fwd, lf, 0.0), jnp.where(row_fwd, 0.0, lf)], axis=1)
        bc = _split_dot(lf2, tri_ref[...])
        gs = ig - bc
        btot = jnp.where(row_fwd, bcast(bc[:, L - 1:L]), bcast(bc[:, 0:1]))
        gmax = bcast(jnp.max(gs, axis=1, keepdims=True))
        gates.append((bc, gs, btot, gmax, jnp.exp(gs - gmax)))

    scores, local = {}, {}
    for u in range(n_sub):
        w_st = gates[u][4]
        for p in range(N_HD):
            qt, k, vt, _, _, _ = operands(u, p)
            scores[u, p] = _dot(k, qt)
            vaug = jnp.concatenate([vt, one_row], axis=0)
            vw = (vaug.astype(F32) * w_st[p:p + 1, :]).astype(BF16)
            local[u, p] = _dot(vw, k)

    for u in range(n_sub):
        bc, gs, btot, gmax, _ = gates[u]
        m_prev = m_st[...]
        m_loc = btot + gmax
        m_new = jnp.maximum(btot + m_prev, m_loc)
        dec = jnp.exp(btot + m_prev - m_new)
        inj = jnp.exp(m_loc - m_new)
        m_st[...] = m_new
        for p in range(N_HD):
            qt, k, vt, dst, hs, tl = operands(u, p)
            seen = (src_row <= out_col) if p < MLSTM_HEADS else (src_row >= out_col)
            gs_seen = jnp.where(seen, jnp.broadcast_to(gs[p:p + 1, :], (L, L)).T, -jnp.inf)
            xm = jnp.maximum(m_prev[p:p + 1, :], jnp.max(gs_seen, axis=0, keepdims=True))
            pt = (scores[u, p] * jnp.exp(gs_seen - xm)).astype(BF16)
            qw = (qt.astype(F32) * jnp.exp(m_prev[p:p + 1, :] - xm)).astype(BF16)
            vaug = jnp.concatenate([vt, one_row], axis=0)
            c_prev = c_st[p]
            lhs = jnp.concatenate([vaug, c_prev.astype(BF16)], axis=1)
            rhs = jnp.concatenate([pt, qw], axis=0)
            o = _dot(lhs, rhs)
            den = jnp.maximum(jnp.abs(o[hd:hd + 1]), jnp.exp(-(bc[p:p + 1, :] + xm)))
            dst[hs, tl] = (o[:hd] * (1.0 / den)).astype(dst.dtype)
            c_st[p] = dec[p:p + 1, :] * c_prev + inj[p:p + 1, :] * local[u, p]

    @pl.when(j == pl.num_programs(1) - 1)
    def _():
        cfin_ref[...] = c_st[...]
        mfin_ref[...] = m_st[...]


def _mlstm(qt, k, vt, gt, bias, c0, m0):
    b, c, s = qt.shape
    tb = min(SCAN_CHUNKS * CHUNK, s)
    nb = s // tb
    tri = _tri_ones()
    fwd_t = lambda bi, j: (bi, 0, j)
    bwd_t = lambda bi, j: (bi, 0, nb - 1 - j)
    fwd_n = lambda bi, j: (bi, j, 0)
    bwd_n = lambda bi, j: (bi, nb - 1 - j, 0)
    st_c = pl.BlockSpec((None, N_HD, STATE_ROWS, MLSTM_HD), lambda bi, j: (bi, 0, 0, 0))
    st_m = pl.BlockSpec((None, N_HD, CHUNK), lambda bi, j: (bi, 0, 0))
    side = lambda ti, ni: [pl.BlockSpec((None, c, tb), ti), pl.BlockSpec((None, tb, c), ni),
                           pl.BlockSpec((None, c, tb), ti), pl.BlockSpec((None, GATE_ROWS, tb), ti)]
    return pl.pallas_call(
        _mlstm_kernel,
        grid=(b, nb),
        in_specs=side(fwd_t, fwd_n) + side(bwd_t, bwd_n) + [_full(bias.shape), _full(tri.shape), st_c, st_m],
        out_specs=[pl.BlockSpec((None, c, tb), fwd_t), pl.BlockSpec((None, c, tb), bwd_t), st_c, st_m],
        out_shape=[jax.ShapeDtypeStruct((b, c, s), BF16),
                   jax.ShapeDtypeStruct((b, c, s), BF16),
                   jax.ShapeDtypeStruct(c0.shape, F32),
                   jax.ShapeDtypeStruct(m0.shape, F32)],
        scratch_shapes=[pltpu.VMEM((N_HD, STATE_ROWS, MLSTM_HD), F32),
                        pltpu.VMEM((N_HD, CHUNK), F32)],
        compiler_params=_cparams(2, 56),
        name="mlstm",
    )(qt, k, vt, gt, qt, k, vt, gt, bias, tri, c0, m0)


DFT_GROUP = SUBLANES


def _dft_tables(s, n2):
    n1 = s // n2
    k1 = np.arange(n1, dtype=np.float64)
    s1 = np.arange(n1, dtype=np.float64)
    ma = np.zeros((n2, 2 * n1, 2 * n1), np.float64)
    for s2 in range(n2):
        th = 2.0 * np.pi * np.outer(k1, s1 * n2 + s2) / s
        c, sn = np.cos(th), np.sin(th)
        ma[s2] = np.block([[c, sn], [-sn, c]])
    k2 = np.arange(n2, dtype=np.float64)
    ph = 2.0 * np.pi * np.outer(k2, k2) / n2
    eye = np.eye(DFT_GROUP)
    scale = 1.0 / math.sqrt(s)
    lb = np.einsum("ksr,ab->kasrb", np.stack([np.cos(ph), np.sin(ph)], axis=-1) * scale, eye)
    lb = lb.reshape(n2 * DFT_GROUP, n2 * 2 * DFT_GROUP)
    return jnp.asarray(ma, F32).astype(BF16), jnp.asarray(lb, F32).astype(BF16)


def _seq_dft_kernel(n_a, z_ref, ma_ref, lb_ref, y_ref, t_scr):
    j = pl.program_id(1)
    g = DFT_GROUP
    n2 = t_scr.shape[0]
    n1 = t_scr.shape[2]

    @pl.when(j < n_a)
    def _():
        for s in range(g):
            zs = z_ref[:, s, :]
            rhs = jnp.concatenate([zs[:, :D_FOURIER], zs[:, D_FOURIER:]], axis=0).astype(BF16)
            t_scr[j * g + s] = _dot(ma_ref[s], rhs).reshape(2, n1, D_FOURIER)

    @pl.when(j >= n_a)
    def _():
        parts = []
        for q in range(y_ref.shape[1] // g):
            k0 = pl.multiple_of((j - n_a) * y_ref.shape[1] + q * g, g)
            rhs = t_scr[:, :, pl.ds(k0, g), :].reshape(n2 * 2 * g, D_FOURIER).astype(BF16)
            parts.append(_dot(lb_ref[...], rhs).reshape(n2, g, D_FOURIER))
        y_ref[...] = jnp.concatenate(parts, axis=1).astype(y_ref.dtype)


def _seq_dft(z, n2=64):
    b, s, w = z.shape
    n1 = s // n2
    g = DFT_GROUP
    n_a = n2 // g
    gb = 2 * PACKED_SUBLANES
    n_b = n1 // gb
    ma, lb = _dft_tables(s, n2)
    y = pl.pallas_call(
        functools.partial(_seq_dft_kernel, n_a),
        grid=(b, n_a + n_b),
        in_specs=[pl.BlockSpec((None, n1, g, w), lambda bi, j: (bi, 0, jnp.minimum(j, n_a - 1), 0)),
                  pl.BlockSpec((g, 2 * n1, 2 * n1), lambda bi, j: (jnp.minimum(j, n_a - 1), 0, 0)),
                  _full(lb.shape)],
        out_specs=pl.BlockSpec((None, n2, gb, D_FOURIER), lambda bi, j: (bi, 0, jnp.maximum(j - n_a, 0), 0)),
        out_shape=jax.ShapeDtypeStruct((b, n2, n1, D_FOURIER), BF16),
        scratch_shapes=[pltpu.VMEM((n2, 2, n1, D_FOURIER), F32)],
        compiler_params=_cparams(2, 60),
        name="seq_dft",
    )(z.reshape(b, n1, n2, w), ma, lb)
    return y.reshape(b, s, D_FOURIER)


def _dense_dft_kernel(z_ref, ld_ref, y_ref):
    rhs = jnp.concatenate([z_ref[:, :D_FOURIER], z_ref[:, D_FOURIER:]], axis=0).astype(BF16)
    y_ref[...] = _dot(ld_ref[...], rhs).astype(y_ref.dtype)


def _dense_dft(z):
    b, s, w = z.shape
    k = np.arange(s, dtype=np.float64)
    th = 2.0 * np.pi * np.outer(k, k) / s
    ld = jnp.asarray(np.concatenate([np.cos(th), np.sin(th)], axis=1) / math.sqrt(s), F32).astype(BF16)
    return pl.pallas_call(
        _dense_dft_kernel,
        grid=(b,),
        in_specs=[pl.BlockSpec((None, s, w), lambda bi: (bi, 0, 0)), _full(ld.shape)],
        out_specs=pl.BlockSpec((None, s, D_FOURIER), lambda bi: (bi, 0, 0)),
        out_shape=jax.ShapeDtypeStruct((b, s, D_FOURIER), BF16),
        compiler_params=_cparams(1, 32),
        name="dft_dense",
    )(z, ld)


def _mixer_out(x_ref, yf_ref, htf_ref, htb_ref, zt_ref, mod, nw_ref, hnw_ref, wom_ref):
    _, _, gate = _mod3(mod, 1)
    hd = MLSTM_HD
    hs = htf_ref[...].astype(F32) + htb_ref[...].astype(F32)
    parts = []
    for h in range(MLSTM_HEADS):
        seg = hs[h * hd:(h + 1) * hd]
        mu = jnp.mean(seg, axis=0, keepdims=True)
        cen = seg - mu
        var = jnp.mean(cen * cen, axis=0, keepdims=True)
        parts.append(cen * lax.rsqrt(var + EPS))
    hn = jnp.concatenate(parts, axis=0) * hnw_ref[...]
    ymt = (hn * jax.nn.sigmoid(zt_ref[...].astype(F32))).astype(BF16)
    y = (_dot(yf_ref[...], wom_ref[:D_FOURIER])
         + lax.dot_general(ymt, wom_ref[D_FOURIER:], TN, preferred_element_type=F32))
    return x_ref[...] + gate * _rms(y, nw_ref[3:4])


def _combine_ffn_kernel(x_ref, yf_ref, htf_ref, htb_ref, zt_ref, mod_ref, nw_ref, hnw_ref, wom_ref,
                        wi_ref, wo_ref, o_ref, a_ref):
    mod = mod_ref[...]
    x2 = _mixer_out(x_ref, yf_ref, htf_ref, htb_ref, zt_ref, mod, nw_ref, hnw_ref, wom_ref)
    o_ref[...] = _ffn_body(x2, mod, 2, nw_ref, wi_ref, wo_ref, a_ref)


def _combine_ffn(x, yf, htf, htb, zt, mod, l, norm_w, hnw, w_mix, ff_in, ff_out):
    b, s, d = x.shape
    tm = min(1024, s)
    tok = lambda w: pl.BlockSpec((None, tm, w), lambda bi, i: (bi, i, 0))
    tr = pl.BlockSpec((None, D_MLSTM, tm), lambda bi, i: (bi, 0, i))
    hnw_b = jnp.broadcast_to(hnw.reshape(D_MLSTM, 1), (D_MLSTM, tm))
    return pl.pallas_call(
        _combine_ffn_kernel,
        grid=(b, s // tm),
        in_specs=[tok(d), tok(D_FOURIER), tr, tr, tr,
                  pl.BlockSpec((None, N_MOD, d), _mod_index(mod)),
                  _resident(norm_w.shape, (l,)), _resident(hnw_b.shape), _resident(w_mix.shape, (l,)),
                  _resident(ff_in.shape, (l, 1)), _resident(ff_out.shape, (l, 1))],
        out_specs=tok(d),
        out_shape=jax.ShapeDtypeStruct(x.shape, F32),
        scratch_shapes=[pltpu.VMEM((tm, D_FF), BF16)],
        compiler_params=_cparams(2, 60),
        name="combine_ffn",
    )(x, yf, htf, htb, zt, mod, norm_w, hnw_b, w_mix, ff_in, ff_out)


def _gate_layout(w_gate, gate_b):
    h = MLSTM_HEADS
    wt = w_gate.T
    w = jnp.concatenate([wt[0:h], wt[2 * h:3 * h], wt[h:2 * h], wt[3 * h:4 * h]], axis=0)
    bias = jnp.concatenate([gate_b[0], gate_b[2], gate_b[1], gate_b[3]])
    return w, jnp.broadcast_to(bias.reshape(GATE_ROWS, 1), (GATE_ROWS, CHUNK))


def kernel(x, c, ctx, c_ctx, w_ada, b_ada, norm_w, w_ff_in, w_ff_out, w_in, w_fmix,
           conv_w, conv_b, w_qkv, gate_b, mlstm_norm_w, w_out):
    b = x.shape[0]
    d = D_MODEL
    n_cond = b + 1
    assert n_cond <= SUBLANES
    cc = jnp.concatenate([c, c_ctx[None], jnp.zeros((SUBLANES - n_cond, d), F32)], axis=0)
    mod = _modulation(cc, w_ada, b_ada).reshape(DEPTH, SUBLANES, N_MOD, d)
    g_mix = _fold_fourier(w_fmix)
    w_fm = w_in[:, :, :D_FOURIER + D_MLSTM].astype(BF16)
    ff_in = w_ff_in.astype(BF16)
    ff_out = w_ff_out.astype(BF16)
    w_mix = w_out.astype(BF16)

    xl, xc = x, ctx
    c_zero = jnp.zeros((b, N_HD, STATE_ROWS, MLSTM_HD), F32)
    m_zero = jnp.zeros((b, N_HD, CHUNK), F32)
    for l in range(DEPTH):
        last = l == DEPTH - 1
        mod_l = mod[l, :b]
        mod_c = mod[l, b:b + 1]
        o_z = D_FOURIER + D_MLSTM
        o_g = D_FOURIER + 2 * D_MLSTM
        w_gate, g_bias = _gate_layout(w_in[l, :, o_g:], gate_b[l])
        pw = (w_fm, g_mix, jnp.concatenate([w_in[l, :, o_z:o_g].T, w_gate], axis=0).astype(BF16))
        cw = conv_w[l].reshape(9, D_MLSTM)
        cb = conv_b[l].reshape(1, D_MLSTM)
        qkvw = (jnp.swapaxes(w_qkv[l, 0], 1, 2).astype(BF16), w_qkv[l, 1].astype(BF16),
                jnp.swapaxes(w_qkv[l, 2], 1, 2).astype(BF16))
        hnw = mlstm_norm_w[l]

        xl, z_l, xm_l, zt_l, gt_l = _ffn_inproj(xl, mod_l, l, norm_w, ff_in, ff_out, *pw)
        xc, z_c, xm_c, zt_c, gt_c = _ffn_inproj(xc, mod_c, l, norm_w, ff_in, ff_out, *pw)
        qkv_l = _convqkv(xm_l, GRID_W, cw, cb, *qkvw)
        qkv_c = _convqkv(xm_c, xm_c.shape[1], cw, cb, *qkvw)
        htf_c, htb_c, c_ctx_st, m_ctx_st = _mlstm(*qkv_c, gt_c, g_bias, c_zero, m_zero)
        htf_l, htb_l, _, _ = _mlstm(*qkv_l, gt_l, g_bias, c_ctx_st, m_ctx_st)
        yf_l = _seq_dft(z_l)
        xl = _combine_ffn(xl, yf_l, htf_l, htb_l, zt_l, mod_l, l, norm_w, hnw, w_mix, ff_in, ff_out)
        if not last:
            yf_c = _dense_dft(z_c)
            xc = _combine_ffn(xc, yf_c, htf_c, htb_c, zt_c, mod_c, l, norm_w, hnw, w_mix, ff_in, ff_out)
    return xl
```

```python
import functools
import math

import numpy as np
import jax
import jax.numpy as jnp
from jax import lax
from jax.experimental import pallas as pl
from jax.experimental.pallas import tpu as pltpu

D_MODEL = 1024
DEPTH = 2
GRID_W = 64
D_FOURIER = 512
FOURIER_GROUPS = 8
FOURIER_GW = 64
D_MLSTM = 512
MLSTM_HEADS = 4
MLSTM_HD = 128
CHUNK = 128
D_FF = 2816
N_MOD = 9
FFN_RES = 0.5
EPS = 1e-6

LANES = 128
SUBLANES = 8
PACKED_SUBLANES = 16
MXU_DIM = 256
VMEM_BYTES = 64 * 1024 * 1024
N_HD = 2 * MLSTM_HEADS
STATE_ROWS = MLSTM_HD + PACKED_SUBLANES
GATE_ROWS = 2 * N_HD
SCAN_CHUNKS = 16

F32 = jnp.float32
BF16 = jnp.bfloat16
NT = (((1,), (1,)), ((), ()))
TN = (((0,), (0,)), ((), ()))


def _cparams(n_grid, vmem_mb):
    return pltpu.CompilerParams(
        dimension_semantics=("arbitrary",) * n_grid,
        vmem_limit_bytes=min(vmem_mb * 1024 * 1024, VMEM_BYTES - 4 * 1024 * 1024))


def _dot(a, b):
    return jnp.dot(a, b, preferred_element_type=F32)


def _dot_nt(a, b):
    return lax.dot_general(a, b, NT, preferred_element_type=F32)


def _rms(x, w):
    return x * lax.rsqrt(jnp.mean(x * x, axis=-1, keepdims=True) + EPS) * w


def _mod3(m, j):
    return m[3 * j:3 * j + 1], m[3 * j + 1:3 * j + 2], m[3 * j + 2:3 * j + 3]


def _mod_index(mod):
    if mod.shape[0] == 1:
        return lambda b, i: (0, 0, 0)
    return lambda b, i: (b, 0, 0)


def _full(shape):
    return pl.BlockSpec(shape, lambda *_: (0,) * len(shape))


def _resident(shape, lead=()):
    rest = tuple(shape[len(lead):])
    index = tuple(lead) + (0,) * len(rest)
    return pl.BlockSpec((None,) * len(lead) + rest, lambda *_: index, pipeline_mode=pl.Buffered(1))


def _mod_kernel(c_ref, w_ref, b_ref, o_ref):
    c = c_ref[...]
    s = (c * jax.nn.sigmoid(c)).astype(BF16)
    o_ref[...] = _dot(s, w_ref[...].astype(BF16)) + b_ref[...]


def _modulation(cc, w_ada, b_ada):
    d = D_MODEL
    return pl.pallas_call(
        _mod_kernel,
        grid=(DEPTH, N_MOD),
        in_specs=[pl.BlockSpec((SUBLANES, d), lambda l, n: (0, 0)),
                  pl.BlockSpec((None, d, d), lambda l, n: (l, 0, n)),
                  pl.BlockSpec((None, 1, d), lambda l, n: (l, 0, n))],
        out_specs=pl.BlockSpec((None, SUBLANES, d), lambda l, n: (l, 0, n)),
        out_shape=jax.ShapeDtypeStruct((DEPTH, SUBLANES, N_MOD * d), F32),
        compiler_params=_cparams(2, 32),
        name="modulation",
    )(cc, w_ada, b_ada.reshape(DEPTH, 1, N_MOD * d))


FF_CHUNK = MXU_DIM
N_FF_CHUNKS = D_FF // FF_CHUNK


def _ffn_prenorm(x, mod, j, nw_ref):
    shift, scale, _ = _mod3(mod, j)
    return (_rms(x, nw_ref[2 * j:2 * j + 1]) * (1.0 + scale) + shift).astype(BF16)


def _ffn_chunks(h, wi_ref, a_ref, lo, hi):
    for c in range(lo, hi):
        sl = slice(c * FF_CHUNK, (c + 1) * FF_CHUNK)
        g = _dot(h, wi_ref[:, sl])
        u = _dot(h, wi_ref[:, D_FF + c * FF_CHUNK:D_FF + (c + 1) * FF_CHUNK])
        a_ref[:, sl] = (g * jax.nn.sigmoid(g) * u).astype(BF16)


def _ffn_finish(x, mod, j, nw_ref, wo_ref, a_ref):
    _, _, gate = _mod3(mod, j)
    y = _dot(a_ref[...], wo_ref[...])
    return x + FFN_RES * gate * _rms(y, nw_ref[2 * j + 1:2 * j + 2])


def _ffn_body(x, mod, j, nw_ref, wi_ref, wo_ref, a_ref):
    _ffn_chunks(_ffn_prenorm(x, mod, j, nw_ref), wi_ref, a_ref, 0, N_FF_CHUNKS)
    return _ffn_finish(x, mod, j, nw_ref, wo_ref, a_ref)


def _fold_kernel(bdw_ref, bdc_ref, bds_ref, o_ref):
    hi = lax.Precision.HIGHEST
    bdw = bdw_ref[...]
    t = MXU_DIM
    for part, dft_ref in enumerate((bdc_ref, bds_ref)):
        g = jnp.dot(dft_ref[...], bdw, precision=hi, preferred_element_type=F32)
        for i in range(D_FOURIER // t):
            o_ref[part * (D_FOURIER // t) + i] = g[i * t:(i + 1) * t, i * t:(i + 1) * t].astype(BF16)


def _channel_dft_blocks():
    c = np.arange(FOURIER_GW, dtype=np.float64)
    ang = 2.0 * np.pi * np.outer(c, c) / FOURIER_GW
    eye = np.eye(FOURIER_GROUPS)
    scale = 1.0 / math.sqrt(FOURIER_GW)
    bdc = np.kron(eye, np.cos(ang) * scale)
    bds = np.kron(eye, -np.sin(ang) * scale)
    return jnp.asarray(bdc, F32), jnp.asarray(bds, F32)


N_MIX_TILES = 2 * D_FOURIER // MXU_DIM


def _fold_fourier(w_fmix):
    eye = jnp.eye(FOURIER_GROUPS, dtype=F32)
    bdw = jnp.einsum("gh,lgcd->lgchd", eye, w_fmix).reshape(DEPTH, D_FOURIER, D_FOURIER)
    bdc, bds = _channel_dft_blocks()
    sq = pl.BlockSpec((D_FOURIER, D_FOURIER), lambda l: (0, 0))
    return pl.pallas_call(
        _fold_kernel,
        grid=(DEPTH,),
        in_specs=[pl.BlockSpec((None, D_FOURIER, D_FOURIER), lambda l: (l, 0, 0)), sq, sq],
        out_specs=pl.BlockSpec((None, N_MIX_TILES, MXU_DIM, MXU_DIM), lambda l: (l, 0, 0, 0)),
        out_shape=jax.ShapeDtypeStruct((DEPTH, N_MIX_TILES, MXU_DIM, MXU_DIM), BF16),
        compiler_params=_cparams(1, 32),
        name="fold_fourier",
    )(bdw, bdc, bds)


def _ffn_inproj_kernel(x_ref, mod_ref, nw_ref, wi_ref, wo_ref, wfm_ref, g_ref, wt_ref,
                       o_ref, z_ref, xm_ref, zt_ref, gt_ref, a_ref):
    mod = mod_ref[...]
    x1 = _ffn_body(x_ref[...], mod, 0, nw_ref, wi_ref, wo_ref, a_ref)
    o_ref[...] = x1
    shift, scale, _ = _mod3(mod, 1)
    h = (_rms(x1, nw_ref[2:3]) * (1.0 + scale) + shift).astype(BF16)
    u = _dot(h, wfm_ref[...])
    xm_ref[...] = u[:, D_FOURIER:]
    xf = u[:, :D_FOURIER].astype(BF16)
    t = MXU_DIM
    n_src = D_FOURIER // t
    for i in range(N_MIX_TILES):
        src = xf[:, (i % n_src) * t:(i % n_src + 1) * t]
        z_ref[:, i * t:(i + 1) * t] = _dot(src, g_ref[i])
    tt = _dot_nt(wt_ref[...], h)
    zt_ref[...] = tt[:D_MLSTM].astype(zt_ref.dtype)
    gt_ref[...] = tt[D_MLSTM:]


def _ffn_inproj(x, mod, l, norm_w, ff_in, ff_out, wfm, g_mix, wt):
    b, s, d = x.shape
    tm = min(1024, s)
    tok = lambda w: pl.BlockSpec((None, tm, w), lambda bi, i: (bi, i, 0))
    tr = lambda r: pl.BlockSpec((None, r, tm), lambda bi, i: (bi, 0, i))
    return pl.pallas_call(
        _ffn_inproj_kernel,
        grid=(b, s // tm),
        in_specs=[tok(d), pl.BlockSpec((None, N_MOD, d), _mod_index(mod)), _resident(norm_w.shape, (l,)),
                  _resident(ff_in.shape, (l, 0)), _resident(ff_out.shape, (l, 0)),
                  _resident(wfm.shape, (l,)), _resident(g_mix.shape, (l,)), _resident(wt.shape)],
        out_specs=[tok(d), tok(2 * D_FOURIER), tok(D_MLSTM), tr(D_MLSTM), tr(GATE_ROWS)],
        out_shape=[jax.ShapeDtypeStruct(x.shape, F32),
                   jax.ShapeDtypeStruct((b, s, 2 * D_FOURIER), F32),
                   jax.ShapeDtypeStruct((b, s, D_MLSTM), F32),
                   jax.ShapeDtypeStruct((b, D_MLSTM, s), BF16),
                   jax.ShapeDtypeStruct((b, GATE_ROWS, s), F32)],
        scratch_shapes=[pltpu.VMEM((tm, D_FF), BF16)],
        compiler_params=_cparams(2, 60),
        name="ffn_inproj",
    )(x, mod, norm_w, ff_in, ff_out, wfm, g_mix, wt)


def _convqkv_kernel(width, rows, halo, prev_ref, cur_ref, next_ref, cw_ref, cb_ref,
                    wqt_ref, wk_ref, wvt_ref, qt_ref, k_ref, vt_ref):
    i = pl.program_id(1)
    tm = cur_ref.shape[0]
    cur = cur_ref[...]
    prev = jnp.where(i > 0, prev_ref[...], 0.0)
    nxt = jnp.where(i < pl.num_programs(1) - 1, next_ref[...], 0.0)
    ext = jnp.concatenate([prev, cur, nxt], axis=0)
    n = ext.shape[0]
    r = lax.broadcasted_iota(jnp.int32, ext.shape, 0)
    col = (r + (width - halo % width)) & (width - 1)
    left = jnp.where(col == 0, 0.0, pltpu.roll(ext, 1, axis=0))
    right = jnp.where(col == width - 1, 0.0, pltpu.roll(ext, n - 1, axis=0))
    taps = (left, ext, right)
    cw = cw_ref[...]
    acc = jnp.zeros((tm, cur.shape[1]), F32) + cb_ref[...]
    for dy in (-1, 0, 1):
        if rows == 1 and dy != 0:
            continue
        lo = halo + dy * width
        for dx in (-1, 0, 1):
            k = (dy + 1) * 3 + (dx + 1)
            acc = acc + cw[k:k + 1] * taps[dx + 1][lo:lo + tm]
    cv = (acc * jax.nn.sigmoid(acc)).astype(BF16)
    xm = cur.astype(BF16)
    hd = MLSTM_HD
    for h in range(MLSTM_HEADS):
        sl = slice(h * hd, (h + 1) * hd)
        qt_ref[sl, :] = _dot_nt(wqt_ref[h], cv[:, sl]).astype(BF16)
        k_ref[:, sl] = (_dot(cv[:, sl], wk_ref[h]) * hd ** -0.5).astype(BF16)
        vt_ref[sl, :] = _dot_nt(wvt_ref[h], xm[:, sl]).astype(BF16)


def _convqkv(xm, width, cw, cb, wqt, wk, wvt):
    b, s, c = xm.shape
    rows = s // width
    tm = min(1024, s)
    halo = LANES if rows > 1 else SUBLANES
    assert tm % width == 0 and tm % halo == 0 and (rows == 1 or halo > width)
    assert width & (width - 1) == 0
    per = tm // halo
    nblk = s // halo
    kern = functools.partial(_convqkv_kernel, width, rows, halo)
    tr = pl.BlockSpec((None, c, tm), lambda bi, i: (bi, 0, i))
    return pl.pallas_call(
        kern,
        grid=(b, s // tm),
        in_specs=[pl.BlockSpec((None, halo, c), lambda bi, i: (bi, jnp.maximum(i * per - 1, 0), 0)),
                  pl.BlockSpec((None, tm, c), lambda bi, i: (bi, i, 0)),
                  pl.BlockSpec((None, halo, c), lambda bi, i: (bi, jnp.minimum((i + 1) * per, nblk - 1), 0)),
                  _full(cw.shape), _full(cb.shape), _full(wqt.shape), _full(wk.shape), _full(wvt.shape)],
        out_specs=[tr, pl.BlockSpec((None, tm, c), lambda bi, i: (bi, i, 0)), tr],
        out_shape=[jax.ShapeDtypeStruct((b, c, s), BF16),
                   jax.ShapeDtypeStruct((b, s, c), BF16),
                   jax.ShapeDtypeStruct((b, c, s), BF16)],
        compiler_params=_cparams(2, 32),
        name="convqkv",
    )(xm, xm, xm, cw, cb, wqt, wk, wvt)


def _tri_ones():
    u = np.arange(CHUNK)[:, None]
    t = np.arange(CHUNK)[None, :]
    return jnp.asarray(np.concatenate([u <= t, u >= t], axis=0), BF16)


def _split_dot(x, w):
    hi = x.astype(BF16)
    r1 = x - hi.astype(F32)
    mid = r1.astype(BF16)
    lo = (r1 - mid.astype(F32)).astype(BF16)
    return _dot(hi, w) + _dot(mid, w) + _dot(lo, w)


def _mlstm_kernel(qtf_ref, kf_ref, vtf_ref, gtf_ref, qtb_ref, kb_ref, vtb_ref, gtb_ref,
                  bias_ref, tri_ref, c0_ref, m0_ref, htf_ref, htb_ref, cfin_ref, mfin_ref, c_st, m_st):
    j = pl.program_id(1)
    L = CHUNK
    hd = MLSTM_HD
    n_sub = gtf_ref.shape[1] // L

    @pl.when(j == 0)
    def _():
        c_st[...] = c0_ref[...]
        m_st[...] = m0_ref[...]

    row_fwd = lax.broadcasted_iota(jnp.int32, (N_HD, L), 0) < MLSTM_HEADS
    src_row = lax.broadcasted_iota(jnp.int32, (L, L), 0)
    out_col = lax.broadcasted_iota(jnp.int32, (L, L), 1)
    one_row = jnp.where(lax.broadcasted_iota(jnp.int32, (PACKED_SUBLANES, L), 0) == 0, 1.0, 0.0).astype(BF16)
    bcast = lambda col: jnp.broadcast_to(col, (N_HD, L))

    def operands(u, p):
        d, hh = divmod(p, MLSTM_HEADS)
        hs = slice(hh * hd, (hh + 1) * hd)
        if d == 0:
            tl = slice(u * L, (u + 1) * L)
            return qtf_ref[hs, tl], kf_ref[tl, hs], vtf_ref[hs, tl], htf_ref, hs, tl
        tl = slice((n_sub - 1 - u) * L, (n_sub - u) * L)
        return qtb_ref[hs, tl], kb_ref[tl, hs], vtb_ref[hs, tl], htb_ref, hs, tl

    gates = []
    for u in range(n_sub):
        fl = slice(u * L, (u + 1) * L)
        bl = slice((n_sub - 1 - u) * L, (n_sub - u) * L)
        ig = jnp.where(row_fwd, gtf_ref[:N_HD, fl], gtb_ref[:N_HD, bl]) + bias_ref[:N_HD]
        fp = jnp.where(row_fwd, gtf_ref[N_HD:, fl], gtb_ref[N_HD:, bl]) + bias_ref[N_HD:]
        lf = jax.nn.log_sigmoid(fp)
        lf2 = jnp.concatenate([jnp.where(row_fwd, lf, 0.0), jnp.where(row_fwd, 0.0, lf)], axis=1)
        bc = _split_dot(lf2, tri_ref[...])
        gs = ig - bc
        btot = jnp.where(row_fwd, bcast(bc[:, L - 1:L]), bcast(bc[:, 0:1]))
        gmax = bcast(jnp.max(gs, axis=1, keepdims=True))
        gates.append((bc, gs, btot, gmax, jnp.exp(gs - gmax)))

    scores, local = {}, {}
    for u in range(n_sub):
        w_st = gates[u][4]
        for p in range(N_HD):
            qt, k, vt, _, _, _ = operands(u, p)
            scores[u, p] = _dot(k, qt)
            vaug = jnp.concatenate([vt, one_row], axis=0)
            vw = (vaug.astype(F32) * w_st[p:p + 1, :]).astype(BF16)
            local[u, p] = _dot(vw, k)

    for u in range(n_sub):
        bc, gs, btot, gmax, _ = gates[u]
        m_prev = m_st[...]
        m_loc = btot + gmax
        m_new = jnp.maximum(btot + m_prev, m_loc)
        dec = jnp.exp(btot + m_prev - m_new)
        inj = jnp.exp(m_loc - m_new)
        m_st[...] = m_new
        for p in range(N_HD):
            qt, k, vt, dst, hs, tl = operands(u, p)
            seen = (src_row <= out_col) if p < MLSTM_HEADS else (src_row >= out_col)
            gs_seen = jnp.where(seen, jnp.broadcast_to(gs[p:p + 1, :], (L, L)).T, -jnp.inf)
            xm = jnp.maximum(m_prev[p:p + 1, :], jnp.max(gs_seen, axis=0, keepdims=True))
            pt = (scores[u, p] * jnp.exp(gs_seen - xm)).astype(BF16)
            qw = (qt.astype(F32) * jnp.exp(m_prev[p:p + 1, :] - xm)).astype(BF16)
            vaug = jnp.concatenate([vt, one_row], axis=0)
            c_prev = c_st[p]
            lhs = jnp.concatenate([vaug, c_prev.astype(BF16)], axis=1)
            rhs = jnp.concatenate([pt, qw], axis=0)
            o = _dot(lhs, rhs)
            den = jnp.maximum(jnp.abs(o[hd:hd + 1]), jnp.exp(-(bc[p:p + 1, :] + xm)))
            dst[hs, tl] = (o[:hd] * (1.0 / den)).astype(dst.dtype)
            c_st[p] = dec[p:p + 1, :] * c_prev + inj[p:p + 1, :] * local[u, p]

    @pl.when(j == pl.num_programs(1) - 1)
    def _():
        cfin_ref[...] = c_st[...]
        mfin_ref[...] = m_st[...]


def _mlstm(qt, k, vt, gt, bias, c0, m0):
    b, c, s = qt.shape
    tb = min(SCAN_CHUNKS * CHUNK, s)
    nb = s // tb
    tri = _tri_ones()
    fwd_t = lambda bi, j: (bi, 0, j)
    bwd_t = lambda bi, j: (bi, 0, nb - 1 - j)
    fwd_n = lambda bi, j: (bi, j, 0)
    bwd_n = lambda bi, j: (bi, nb - 1 - j, 0)
    st_c = pl.BlockSpec((None, N_HD, STATE_ROWS, MLSTM_HD), lambda bi, j: (bi, 0, 0, 0))
    st_m = pl.BlockSpec((None, N_HD, CHUNK), lambda bi, j: (bi, 0, 0))
    side = lambda ti, ni: [pl.BlockSpec((None, c, tb), ti), pl.BlockSpec((None, tb, c), ni),
                           pl.BlockSpec((None, c, tb), ti), pl.BlockSpec((None, GATE_ROWS, tb), ti)]
    return pl.pallas_call(
        _mlstm_kernel,
        grid=(b, nb),
        in_specs=side(fwd_t, fwd_n) + side(bwd_t, bwd_n) + [_full(bias.shape), _full(tri.shape), st_c, st_m],
        out_specs=[pl.BlockSpec((None, c, tb), fwd_t), pl.BlockSpec((None, c, tb), bwd_t), st_c, st_m],
        out_shape=[jax.ShapeDtypeStruct((b, c, s), BF16),
                   jax.ShapeDtypeStruct((b, c, s), BF16),
                   jax.ShapeDtypeStruct(c0.shape, F32),
                   jax.ShapeDtypeStruct(m0.shape, F32)],
        scratch_shapes=[pltpu.VMEM((N_HD, STATE_ROWS, MLSTM_HD), F32),
                        pltpu.VMEM((N_HD, CHUNK), F32)],
        compiler_params=_cparams(2, 60),
        name="mlstm",
    )(qt, k, vt, gt, qt, k, vt, gt, bias, tri, c0, m0)


DFT_GROUP = SUBLANES


def _dft_tables(s, n2):
    n1 = s // n2
    k1 = np.arange(n1, dtype=np.float64)
    s1 = np.arange(n1, dtype=np.float64)
    ma = np.zeros((n2, 2 * n1, 2 * n1), np.float64)
    for s2 in range(n2):
        th = 2.0 * np.pi * np.outer(k1, s1 * n2 + s2) / s
        c, sn = np.cos(th), np.sin(th)
        ma[s2] = np.block([[c, sn], [-sn, c]])
    k2 = np.arange(n2, dtype=np.float64)
    ph = 2.0 * np.pi * np.outer(k2, k2) / n2
    eye = np.eye(DFT_GROUP)
    scale = 1.0 / math.sqrt(s)
    lb = np.einsum("ksr,ab->kasrb", np.stack([np.cos(ph), np.sin(ph)], axis=-1) * scale, eye)
    lb = lb.reshape(n2 * DFT_GROUP, n2 * 2 * DFT_GROUP)
    return jnp.asarray(ma, F32).astype(BF16), jnp.asarray(lb, F32).astype(BF16)


def _seq_dft_kernel(n_a, z_ref, ma_ref, lb_ref, y_ref, t_scr):
    j = pl.program_id(1)
    g = DFT_GROUP
    n2 = t_scr.shape[0]
    n1 = t_scr.shape[2]

    @pl.when(j < n_a)
    def _():
        for s in range(g):
            zs = z_ref[:, s, :]
            rhs = jnp.concatenate([zs[:, :D_FOURIER], zs[:, D_FOURIER:]], axis=0).astype(BF16)
            t_scr[j * g + s] = _dot(ma_ref[s], rhs).reshape(2, n1, D_FOURIER)

    @pl.when(j >= n_a)
    def _():
        parts = []
        for q in range(y_ref.shape[1] // g):
            k0 = pl.multiple_of((j - n_a) * y_ref.shape[1] + q * g, g)
            rhs = t_scr[:, :, pl.ds(k0, g), :].reshape(n2 * 2 * g, D_FOURIER).astype(BF16)
            parts.append(_dot(lb_ref[...], rhs).reshape(n2, g, D_FOURIER))
        y_ref[...] = jnp.concatenate(parts, axis=1).astype(y_ref.dtype)


def _seq_dft(z, n2=64):
    b, s, w = z.shape
    n1 = s // n2
    g = DFT_GROUP
    n_a = n2 // g
    gb = 2 * PACKED_SUBLANES
    n_b = n1 // gb
    ma, lb = _dft_tables(s, n2)
    y = pl.pallas_call(
        functools.partial(_seq_dft_kernel, n_a),
        grid=(b, n_a + n_b),
        in_specs=[pl.BlockSpec((None, n1, g, w), lambda bi, j: (bi, 0, jnp.minimum(j, n_a - 1), 0)),
                  pl.BlockSpec((g, 2 * n1, 2 * n1), lambda bi, j: (jnp.minimum(j, n_a - 1), 0, 0)),
                  _full(lb.shape)],
        out_specs=pl.BlockSpec((None, n2, gb, D_FOURIER), lambda bi, j: (bi, 0, jnp.maximum(j - n_a, 0), 0)),
        out_shape=jax.ShapeDtypeStruct((b, n2, n1, D_FOURIER), BF16),
        scratch_shapes=[pltpu.VMEM((n2, 2, n1, D_FOURIER), F32)],
        compiler_params=_cparams(2, 60),
        name="seq_dft",
    )(z.reshape(b, n1, n2, w), ma, lb)
    return y.reshape(b, s, D_FOURIER)


def _dense_dft_kernel(z_ref, ld_ref, y_ref):
    rhs = jnp.concatenate([z_ref[:, :D_FOURIER], z_ref[:, D_FOURIER:]], axis=0).astype(BF16)
    y_ref[...] = _dot(ld_ref[...], rhs).astype(y_ref.dtype)


def _dense_dft(z):
    b, s, w = z.shape
    k = np.arange(s, dtype=np.float64)
    th = 2.0 * np.pi * np.outer(k, k) / s
    ld = jnp.asarray(np.concatenate([np.cos(th), np.sin(th)], axis=1) / math.sqrt(s), F32).astype(BF16)
    return pl.pallas_call(
        _dense_dft_kernel,
        grid=(b,),
        in_specs=[pl.BlockSpec((None, s, w), lambda bi: (bi, 0, 0)), _full(ld.shape)],
        out_specs=pl.BlockSpec((None, s, D_FOURIER), lambda bi: (bi, 0, 0)),
        out_shape=jax.ShapeDtypeStruct((b, s, D_FOURIER), BF16),
        compiler_params=_cparams(1, 32),
        name="dft_dense",
    )(z, ld)


def _mixer_out(x_ref, yf_ref, htf_ref, htb_ref, zt_ref, mod, nw_ref, hnw_ref, wom_ref):
    _, _, gate = _mod3(mod, 1)
    hd = MLSTM_HD
    hs = htf_ref[...].astype(F32) + htb_ref[...].astype(F32)
    parts = []
    for h in range(MLSTM_HEADS):
        seg = hs[h * hd:(h + 1) * hd]
        mu = jnp.mean(seg, axis=0, keepdims=True)
        cen = seg - mu
        var = jnp.mean(cen * cen, axis=0, keepdims=True)
        parts.append(cen * lax.rsqrt(var + EPS))
    hn = jnp.concatenate(parts, axis=0) * hnw_ref[...]
    ymt = (hn * jax.nn.sigmoid(zt_ref[...].astype(F32))).astype(BF16)
    y = (_dot(yf_ref[...], wom_ref[:D_FOURIER])
         + lax.dot_general(ymt, wom_ref[D_FOURIER:], TN, preferred_element_type=F32))
    return x_ref[...] + gate * _rms(y, nw_ref[3:4])


def _combine_ffn_kernel(x_ref, yf_ref, htf_ref, htb_ref, zt_ref, mod_ref, nw_ref, hnw_ref, wom_ref,
                        wi_ref, wo_ref, o_ref, a_ref):
    mod = mod_ref[...]
    x2 = _mixer_out(x_ref, yf_ref, htf_ref, htb_ref, zt_ref, mod, nw_ref, hnw_ref, wom_ref)
    o_ref[...] = _ffn_body(x2, mod, 2, nw_ref, wi_ref, wo_ref, a_ref)


def _combine_ffn(x, yf, htf, htb, zt, mod, l, norm_w, hnw, w_mix, ff_in, ff_out):
    b, s, d = x.shape
    tm = min(1024, s)
    tok = lambda w: pl.BlockSpec((None, tm, w), lambda bi, i: (bi, i, 0))
    tr = pl.BlockSpec((None, D_MLSTM, tm), lambda bi, i: (bi, 0, i))
    hnw_b = jnp.broadcast_to(hnw.reshape(D_MLSTM, 1), (D_MLSTM, tm))
    return pl.pallas_call(
        _combine_ffn_kernel,
        grid=(b, s // tm),
        in_specs=[tok(d), tok(D_FOURIER), tr, tr, tr,
                  pl.BlockSpec((None, N_MOD, d), _mod_index(mod)),
                  _resident(norm_w.shape, (l,)), _resident(hnw_b.shape), _resident(w_mix.shape, (l,)),
                  _resident(ff_in.shape, (l, 1)), _resident(ff_out.shape, (l, 1))],
        out_specs=tok(d),
        out_shape=jax.ShapeDtypeStruct(x.shape, F32),
        scratch_shapes=[pltpu.VMEM((tm, D_FF), BF16)],
        compiler_params=_cparams(2, 60),
        name="combine_ffn",
    )(x, yf, htf, htb, zt, mod, norm_w, hnw_b, w_mix, ff_in, ff_out)


def _gate_layout(w_gate, gate_b):
    h = MLSTM_HEADS
    wt = w_gate.T
    w = jnp.concatenate([wt[0:h], wt[2 * h:3 * h], wt[h:2 * h], wt[3 * h:4 * h]], axis=0)
    bias = jnp.concatenate([gate_b[0], gate_b[2], gate_b[1], gate_b[3]])
    return w, jnp.broadcast_to(bias.reshape(GATE_ROWS, 1), (GATE_ROWS, CHUNK))


def kernel(x, c, ctx, c_ctx, w_ada, b_ada, norm_w, w_ff_in, w_ff_out, w_in, w_fmix,
           conv_w, conv_b, w_qkv, gate_b, mlstm_norm_w, w_out):
    b = x.shape[0]
    d = D_MODEL
    n_cond = b + 1
    assert n_cond <= SUBLANES
    cc = jnp.concatenate([c, c_ctx[None], jnp.zeros((SUBLANES - n_cond, d), F32)], axis=0)
    mod = _modulation(cc, w_ada, b_ada).reshape(DEPTH, SUBLANES, N_MOD, d)
    g_mix = _fold_fourier(w_fmix)
    w_fm = w_in[:, :, :D_FOURIER + D_MLSTM].astype(BF16)
    ff_in = w_ff_in.astype(BF16)
    ff_out = w_ff_out.astype(BF16)
    w_mix = w_out.astype(BF16)

    xl, xc = x, ctx
    c_zero = jnp.zeros((b, N_HD, STATE_ROWS, MLSTM_HD), F32)
    m_zero = jnp.zeros((b, N_HD, CHUNK), F32)
    for l in range(DEPTH):
        last = l == DEPTH - 1
        mod_l = mod[l, :b]
        mod_c = mod[l, b:b + 1]
        o_z = D_FOURIER + D_MLSTM
        o_g = D_FOURIER + 2 * D_MLSTM
        w_gate, g_bias = _gate_layout(w_in[l, :, o_g:], gate_b[l])
        pw = (w_fm, g_mix, jnp.concatenate([w_in[l, :, o_z:o_g].T, w_gate], axis=0).astype(BF16))
        cw = conv_w[l].reshape(9, D_MLSTM)
        cb = conv_b[l].reshape(1, D_MLSTM)
        qkvw = (jnp.swapaxes(w_qkv[l, 0], 1, 2).astype(BF16), w_qkv[l, 1].astype(BF16),
                jnp.swapaxes(w_qkv[l, 2], 1, 2).astype(BF16))
        hnw = mlstm_norm_w[l]

        xl, z_l, xm_l, zt_l, gt_l = _ffn_inproj(xl, mod_l, l, norm_w, ff_in, ff_out, *pw)
        xc, z_c, xm_c, zt_c, gt_c = _ffn_inproj(xc, mod_c, l, norm_w, ff_in, ff_out, *pw)
        qkv_l = _convqkv(xm_l, GRID_W, cw, cb, *qkvw)
        qkv_c = _convqkv(xm_c, xm_c.shape[1], cw, cb, *qkvw)
        htf_c, htb_c, c_ctx_st, m_ctx_st = _mlstm(*qkv_c, gt_c, g_bias, c_zero, m_zero)
        htf_l, htb_l, _, _ = _mlstm(*qkv_l, gt_l, g_bias, c_ctx_st, m_ctx_st)
        yf_l = _seq_dft(z_l)
        xl = _combine_ffn(xl, yf_l, htf_l, htb_l, zt_l, mod_l, l, norm_w, hnw, w_mix, ff_in, ff_out)
        if not last:
            yf_c = _dense_dft(z_c)
            xc = _combine_ffn(xc, yf_c, htf_c, htb_c, zt_c, mod_c, l, norm_w, hnw, w_mix, ff_in, ff_out)
    return xl
```
